```python
import jax, jax.numpy as jnp
from jax import lax
import numpy as np

D_MODEL = 4096
BATCH = 2
SEQ = 4096
DEPTH = 1
DEC_BATCH = 4
DEC_SEQ = 2048
PAST_LEN = 128

N_META = 16
MIX_WIDTH = D_MODEL
GLA_WIDTH = MIX_WIDTH // 2
CONV_WIDTH = MIX_WIDTH - GLA_WIDTH
GLA_HEADS = 8
GLA_DV = GLA_WIDTH // GLA_HEADS
GLA_DK = GLA_DV // 2
GLA_KEY_WIDTH = GLA_HEADS * GLA_DK
GATE_RANK = 16
GATE_TAU = 16.0
CHUNK = 64
META_PAD = CHUNK - N_META
CONV_GROUPS = 16
D_FF = 11008
EPS = 1e-6
IN_SIZES = (GLA_KEY_WIDTH, GLA_KEY_WIDTH, GLA_WIDTH, GLA_WIDTH, GATE_RANK, GATE_RANK,
            CONV_WIDTH, CONV_WIDTH, CONV_WIDTH)
IN_COLS = 2 * GLA_KEY_WIDTH + 2 * GLA_WIDTH + 2 * GATE_RANK + 3 * CONV_WIDTH

kernel_name = "hymba_gla_shortconv_encoder"


def _rmsnorm(x, g):
    xf = x.astype(jnp.float32)
    y = xf * lax.rsqrt(jnp.mean(xf * xf, axis=-1, keepdims=True) + EPS)
    return (y * g.astype(jnp.float32)).astype(x.dtype)


def _dwconv3(u, w):
    up = jnp.pad(u, ((0, 0), (1, 1), (0, 0)))
    w = w.astype(u.dtype)
    return up[:, :-2] * w[0] + up[:, 1:-1] * w[1] + up[:, 2:] * w[2]


def _split(p, sizes):
    outs, off = [], 0
    for s in sizes:
        outs.append(p[..., off:off + s])
        off += s
    return outs


def _gla_scan(q, k, v, lg):
    z, b, lp, h, dk = q.shape
    dv = v.shape[-1]
    n = lp // CHUNK

    def chunks(t):
        return t.astype(jnp.float32).reshape(z, b, n, CHUNK, h, t.shape[-1]).transpose(2, 0, 1, 4, 3, 5)

    tri = jnp.tril(jnp.ones((CHUNK, CHUNK), dtype=bool))
    mask = jnp.stack([tri, jnp.tril(tri, -1)])[:, None, None, :, :, None]

    def step(state, inp):
        qc, kc, vc, gc = inp
        bcum = jnp.cumsum(gc, axis=-2)
        diff = bcum[..., :, None, :] - bcum[..., None, :, :]
        decay = jnp.exp(jnp.where(mask, diff, -jnp.inf))
        scores = jnp.einsum('zbhtd,zbhsd,zbhtsd->zbhts', qc, kc, decay)
        out = (jnp.einsum('zbhts,zbhsv->zbhtv', scores, vc)
               + jnp.einsum('zbhtd,zbhdv->zbhtv', qc * jnp.exp(bcum), state))
        blast = bcum[..., -1:, :]
        state = (jnp.exp(blast[..., 0, :])[..., None] * state
                 + jnp.einsum('zbhsd,zbhsv->zbhdv', kc * jnp.exp(blast - bcum), vc))
        return state, out

    s0 = jnp.zeros((z, b, h, dk, dv), jnp.float32)
    _, o = lax.scan(step, s0, (chunks(q), chunks(k), chunks(v), chunks(lg)))
    return o.transpose(1, 2, 0, 4, 3, 5).reshape(z, b, lp, h, dv)


def _layer(x, mix_norm_g, w_in, w_gate2, b_gate2, head_norm_g, conv_mix_w, w_out,
           ffn_norm_g, w_up, ffn_conv_w, ffn_conv_b, w_down):
    bsz, length, _ = x.shape
    h = _rmsnorm(x, mix_norm_g)
    q, k, v, r, a_f, a_b, cb, cc, ch = _split(h @ w_in, IN_SIZES)

    q = q.reshape(bsz, length, GLA_HEADS, GLA_DK) * (GLA_DK ** -0.5)
    k = k.reshape(bsz, length, GLA_HEADS, GLA_DK)
    v = v.reshape(bsz, length, GLA_HEADS, GLA_DV)
    lg_f = (jax.nn.log_sigmoid((a_f @ w_gate2[0] + b_gate2[0]).astype(jnp.float32)) / GATE_TAU
            ).reshape(bsz, length, GLA_HEADS, GLA_DK)
    lg_b = (jax.nn.log_sigmoid((a_b @ w_gate2[1] + b_gate2[1]).astype(jnp.float32)) / GATE_TAU
            ).reshape(bsz, length, GLA_HEADS, GLA_DK)
    pad = ((0, 0), (META_PAD, 0), (0, 0), (0, 0))

    def both(tf, tb):
        return jnp.stack([jnp.pad(tf, pad), jnp.pad(tb, pad)[:, ::-1]])

    o2 = _gla_scan(both(q, q), both(k, k), both(v, v), both(lg_f, lg_b))
    o = (o2[0] + o2[1][:, ::-1])[:, META_PAD:]
    o = _rmsnorm(o, head_norm_g).reshape(bsz, length, GLA_WIDTH).astype(x.dtype) * jax.nn.silu(r)

    conv_out = cb * _dwconv3(cc * ch, conv_mix_w)

    x = x + jnp.concatenate([o, conv_out], axis=-1) @ w_out

    h2 = _rmsnorm(x, ffn_norm_g)
    u, g = jnp.split(h2 @ w_up, 2, axis=-1)
    g = _dwconv3(g, ffn_conv_w) + ffn_conv_b
    return x + (jax.nn.silu(g) * u) @ w_down


def _trunk(x, meta_tokens, mix_norm_g, w_in, w_gate2, b_gate2, head_norm_g, conv_mix_w, w_out,
           ffn_norm_g, w_up, ffn_conv_w, ffn_conv_b, w_down, final_norm_g):
    bsz = x.shape[0]
    meta = jnp.broadcast_to(meta_tokens.astype(x.dtype)[None], (bsz, N_META, x.shape[-1]))
    hs = jnp.concatenate([meta, x], axis=1)
    for l in range(DEPTH):
        hs = _layer(hs, mix_norm_g[l], w_in[l], w_gate2[l], b_gate2[l], head_norm_g[l],
                    conv_mix_w[l], w_out[l], ffn_norm_g[l], w_up[l], ffn_conv_w[l],
                    ffn_conv_b[l], w_down[l])
    hs = _rmsnorm(hs, final_norm_g)
    return hs[:, N_META:]


def setup_inputs(seed: int = 0) -> dict:
    key = jax.random.key(seed)
    ks = jax.random.split(key, 16)
    f32 = jnp.float32
    nrm = lambda k, s: jax.random.normal(k, s, f32)
    return {
        "x_prompt": nrm(ks[0], (BATCH, SEQ, D_MODEL)),
        "x_sample": nrm(ks[1], (DEC_BATCH, DEC_SEQ, D_MODEL)),
        "meta_tokens": nrm(ks[2], (N_META, D_MODEL)),
        "mix_norm_g": 1.0 + 0.01 * nrm(ks[3], (DEPTH, D_MODEL)),
        "w_in": nrm(ks[4], (DEPTH, D_MODEL, IN_COLS)) * D_MODEL ** -0.5,
        "w_gate2": nrm(ks[5], (DEPTH, 2, GATE_RANK, GLA_KEY_WIDTH)) * GATE_RANK ** -0.5,
        "b_gate2": 0.1 * nrm(ks[6], (DEPTH, 2, GLA_KEY_WIDTH)),
        "head_norm_g": 1.0 + 0.01 * nrm(ks[7], (DEPTH, GLA_DV)),
        "conv_mix_w": nrm(ks[8], (DEPTH, 3, CONV_WIDTH)) * 3.0 ** -0.5,
        "w_out": nrm(ks[9], (DEPTH, MIX_WIDTH, D_MODEL)) * MIX_WIDTH ** -0.5,
        "ffn_norm_g": 1.0 + 0.01 * nrm(ks[10], (DEPTH, D_MODEL)),
        "w_up": nrm(ks[11], (DEPTH, D_MODEL, 2 * D_FF)) * D_MODEL ** -0.5,
        "ffn_conv_w": nrm(ks[12], (DEPTH, 3, D_FF)) * 3.0 ** -0.5,
        "ffn_conv_b": 0.01 * nrm(ks[13], (DEPTH, D_FF)),
        "w_down": nrm(ks[14], (DEPTH, D_FF, D_MODEL)) * D_FF ** -0.5,
        "final_norm_g": 1.0 + 0.01 * nrm(ks[15], (D_MODEL,)),
    }


def reference(x_prompt, x_sample, meta_tokens, mix_norm_g, w_in, w_gate2, b_gate2, head_norm_g,
              conv_mix_w, w_out, ffn_norm_g, w_up, ffn_conv_w, ffn_conv_b, w_down, final_norm_g):
    y_prompt = _trunk(x_prompt, meta_tokens, mix_norm_g, w_in, w_gate2, b_gate2, head_norm_g,
                      conv_mix_w, w_out, ffn_norm_g, w_up, ffn_conv_w, ffn_conv_b, w_down,
                      final_norm_g)
    y_sample = _trunk(x_sample, meta_tokens, mix_norm_g, w_in, w_gate2, b_gate2, head_norm_g,
                      conv_mix_w, w_out, ffn_norm_g, w_up, ffn_conv_w, ffn_conv_b, w_down,
                      final_norm_g)
    return (y_prompt, y_sample)
```

```python
import functools

import numpy as np
import jax
import jax.numpy as jnp
from jax import lax
from jax.experimental import pallas as pl
from jax.experimental.pallas import tpu as pltpu

F32 = jnp.float32
BF16 = jnp.bfloat16

D_MODEL = 4096
N_META = 16
GLA_HEADS = 8
GLA_DK = 128
GLA_DV = 256
GLA_KEY_WIDTH = GLA_HEADS * GLA_DK
GLA_WIDTH = GLA_HEADS * GLA_DV
CONV_WIDTH = 2048
GATE_RANK = 16
GATE_TAU = 16.0
D_FF = 11008
EPS = 1e-6

P_COLS = 2 * GLA_KEY_WIDTH + 2 * GLA_WIDTH + 3 * CONV_WIDTH
A_COLS = 2 * GATE_RANK
Q_OFF, K_OFF, V_OFF, R_OFF = 0, GLA_KEY_WIDTH, 2 * GLA_KEY_WIDTH, 2 * GLA_KEY_WIDTH + GLA_WIDTH
CB_OFF = R_OFF + GLA_WIDTH
CC_OFF = CB_OFF + CONV_WIDTH
CH_OFF = CC_OFF + CONV_WIDTH

CHUNK = 256
LEVELS = 8
HALO = 16
CONV_COLS = 256
V7X_VMEM_LIMIT = 56 * 1024 * 1024


def _params(*sem):
    return pltpu.CompilerParams(dimension_semantics=sem, vmem_limit_bytes=V7X_VMEM_LIMIT)


def _rmsnorm_rows(x, g):
    ms = jnp.mean(x * x, axis=-1, keepdims=True)
    return x * lax.rsqrt(ms + EPS) * g


def _silu(x):
    return x * (1.0 / (1.0 + jnp.exp(-x)))


def _in_proj_body(x_ref, g_ref, w_ref, wa_ref, p_ref, a_ref, h_scr, *, rows):
    @pl.when(pl.program_id(1) == 0)
    def _():
        def norm_rows(i, carry):
            sl = pl.ds(pl.multiple_of(i * rows, rows), rows)
            h_scr[sl, :] = _rmsnorm_rows(x_ref[sl, :], g_ref[...]).astype(BF16)
            return carry
        lax.fori_loop(0, x_ref.shape[0] // rows, norm_rows, 0)
        a_ref[...] = jnp.dot(h_scr[...], wa_ref[...], preferred_element_type=F32)

    p_ref[...] = jnp.dot(h_scr[...], w_ref[...], preferred_element_type=F32).astype(BF16)


def _in_proj(x, g, w_main, w_a, *, tm, tn):
    m = x.shape[0]
    return pl.pallas_call(
        functools.partial(_in_proj_body, rows=min(32, tm)),
        grid=(m // tm, P_COLS // tn),
        in_specs=[pl.BlockSpec((tm, D_MODEL), lambda i, j: (i, 0)),
                  pl.BlockSpec((1, D_MODEL), lambda i, j: (0, 0)),
                  pl.BlockSpec((D_MODEL, tn), lambda i, j: (0, j)),
                  pl.BlockSpec((D_MODEL, A_COLS), lambda i, j: (0, 0))],
        out_specs=[pl.BlockSpec((tm, tn), lambda i, j: (i, j)),
                   pl.BlockSpec((tm, A_COLS), lambda i, j: (i, 0))],
        out_shape=[jax.ShapeDtypeStruct((m, P_COLS), BF16),
                   jax.ShapeDtypeStruct((m, A_COLS), F32)],
        scratch_shapes=[pltpu.VMEM((tm, D_MODEL), BF16)],
        compiler_params=_params("parallel", "arbitrary"),
        name="in_proj",
    )(x, g, w_main, w_a)


def _log_decay(a, w2, b2):
    z = jnp.dot(a.astype(BF16), w2, preferred_element_type=F32) + b2
    return (jnp.minimum(z, 0.0) - jnp.log1p(jnp.exp(-jnp.abs(z)))) * (1.0 / GATE_TAU)


def _chunk_intra(q, k, v, lg, lv):
    shape = (CHUNK, 2 * GLA_DK)
    row = lax.broadcasted_iota(jnp.int32, shape, 0)
    isb = lax.broadcasted_iota(jnp.int32, shape, 1) >= GLA_DK
    q2 = jnp.concatenate([q, q], axis=1)
    k2 = jnp.concatenate([k, k], axis=1)
    p = jnp.where(isb, 0.0, lg)
    tb = lg
    scores = jnp.zeros((CHUNK, CHUNK), F32)
    for lvl in range(LEVELS):
        m = 1 << lvl
        upper = (row & m) != 0
        e = jnp.exp(jnp.where(upper, p, tb - p))
        qside = jnp.logical_xor(upper, isb)
        ql = jnp.where(qside, q2 * e, 0.0).astype(BF16)
        kl = jnp.where(qside, 0.0, k2 * e).astype(BF16)
        s = lax.dot_general(ql, kl, (((1,), (1,)), ((), ())), preferred_element_type=F32)
        scores = jnp.where(lv == lvl, s, scores)
        tb_prev = pltpu.roll(tb, m, 0)
        tb_next = pltpu.roll(tb, CHUNK - m, 0)
        p = p + jnp.where(upper, tb_prev, 0.0)
        tb = tb + jnp.where(upper, tb_prev, tb_next)
    vb = v.astype(BF16)
    qk = jnp.sum(q * k, axis=-1, keepdims=True)
    o_part = jnp.dot(scores.astype(BF16), vb, preferred_element_type=F32) + qk * v
    khat = (k2 * jnp.exp(jnp.where(isb, p, tb - p))).astype(BF16)
    qhat = (q2 * jnp.exp(jnp.where(isb, tb - p, p))).astype(BF16)
    dstate = lax.dot_general(vb, khat, (((0,), (0,)), ((), ())), preferred_element_type=F32)
    return o_part, qhat, dstate, jnp.exp(tb[0:1, :])


def _gla_body(q_ref, k_ref, v_ref, r_ref, a_ref, qm_ref, km_ref, vm_ref, rm_ref, am_ref,
              w2_ref, b2_ref, hg_ref, lv_ref, o_ref, om_ref,
              opart_scr, qhat_scr, st_scr, dec_scr):
    n_main = q_ref.shape[0] // CHUNK
    n = n_main + 1
    scale = GLA_DK ** -0.5
    w2 = w2_ref[...]
    b2 = b2_ref[...]
    pad = CHUNK - N_META

    def intra(c, q, k, v, lg):
        o_part, qhat, dstate, dec = _chunk_intra(q * scale, k, v, lg, lv_ref[...])
        opart_scr[c] = o_part
        qhat_scr[c] = qhat
        st_scr[c] = dstate
        dec_scr[c] = dec

    def behind_zeros(x):
        return jnp.concatenate([jnp.zeros((pad, x.shape[1]), F32), x.astype(F32)], axis=0)

    intra(0, behind_zeros(qm_ref[...]), behind_zeros(km_ref[...]), behind_zeros(vm_ref[...]),
          behind_zeros(_log_decay(am_ref[...], w2, b2)))

    def intra_main(c, carry):
        sl = pl.ds(pl.multiple_of(c * CHUNK, CHUNK), CHUNK)
        intra(c + 1, q_ref[sl, :].astype(F32), k_ref[sl, :].astype(F32), v_ref[sl, :].astype(F32),
              _log_decay(a_ref[sl, :], w2, b2))
        return carry
    lax.fori_loop(0, n_main, intra_main, 0)

    def scan_fwd(c, s):
        d = st_scr[c, :, 0:GLA_DK]
        st_scr[c, :, 0:GLA_DK] = s
        return dec_scr[c][:, 0:GLA_DK] * s + d
    lax.fori_loop(0, n, scan_fwd, jnp.zeros((GLA_DV, GLA_DK), F32))

    def scan_bwd(i, s):
        c = n - 1 - i
        d = st_scr[c, :, GLA_DK:]
        st_scr[c, :, GLA_DK:] = s
        return dec_scr[c][:, GLA_DK:] * s + d
    lax.fori_loop(0, n, scan_bwd, jnp.zeros((GLA_DV, GLA_DK), F32))

    def normed(c):
        o = opart_scr[c] + lax.dot_general(qhat_scr[c], st_scr[c].astype(BF16),
                                           (((1,), (1,)), ((), ())), preferred_element_type=F32)
        return _rmsnorm_rows(o, hg_ref[...])

    om_ref[...] = (normed(0)[pad:, :] * _silu(rm_ref[...].astype(F32))).astype(BF16)

    def finish_main(c, carry):
        sl = pl.ds(pl.multiple_of(c * CHUNK, CHUNK), CHUNK)
        o_ref[sl, :] = (normed(c + 1) * _silu(r_ref[sl, :].astype(F32))).astype(BF16)
        return carry
    lax.fori_loop(0, n_main, finish_main, 0)


def _level_matrix():
    t = np.arange(CHUNK)
    x = t[:, None] ^ t[None, :]
    lv = np.full((CHUNK, CHUNK), -1, np.int32)
    nz = x > 0
    lv[nz] = np.floor(np.log2(x[nz])).astype(np.int32)
    return jnp.asarray(lv)


def _gla(p_main, a_main, p_meta, a_meta, w2blk, b2blk, head_g, *, bsz, seq, shared_meta):
    n = seq // CHUNK + 1
    qo, ko = Q_OFF // GLA_DK, K_OFF // GLA_DK
    vo, ro = V_OFF // GLA_DV, R_OFF // GLA_DV
    mrow = (lambda b: 0) if shared_meta else (lambda b: b)
    main = lambda width, off: pl.BlockSpec((seq, width), lambda b, h: (b, off + h))
    meta = lambda width, off: pl.BlockSpec((N_META, width), lambda b, h: (mrow(b), off + h))
    return pl.pallas_call(
        _gla_body,
        grid=(bsz, GLA_HEADS),
        in_specs=[main(GLA_DK, qo), main(GLA_DK, ko), main(GLA_DV, vo), main(GLA_DV, ro),
                  pl.BlockSpec((seq, A_COLS), lambda b, h: (b, 0)),
                  meta(GLA_DK, qo), meta(GLA_DK, ko), meta(GLA_DV, vo), meta(GLA_DV, ro),
                  pl.BlockSpec((N_META, A_COLS), lambda b, h: (mrow(b), 0)),
                  pl.BlockSpec((None, A_COLS, 2 * GLA_DK), lambda b, h: (h, 0, 0)),
                  pl.BlockSpec((None, 1, 2 * GLA_DK), lambda b, h: (h, 0, 0)),
                  pl.BlockSpec((1, GLA_DV), lambda b, h: (0, 0)),
                  pl.BlockSpec((CHUNK, CHUNK), lambda b, h: (0, 0))],
        out_specs=[pl.BlockSpec((seq, GLA_DV), lambda b, h: (b, h)),
                   pl.BlockSpec((N_META, GLA_DV), lambda b, h: (b, h))],
        out_shape=[jax.ShapeDtypeStruct((bsz * seq, GLA_WIDTH), BF16),
                   jax.ShapeDtypeStruct((bsz * N_META, GLA_WIDTH), BF16)],
        scratch_shapes=[pltpu.VMEM((n, CHUNK, GLA_DV), F32),
                        pltpu.VMEM((n, CHUNK, 2 * GLA_DK), BF16),
                        pltpu.VMEM((n, GLA_DV, 2 * GLA_DK), F32),
                        pltpu.VMEM((n, 1, 2 * GLA_DK), F32)],
        compiler_params=_params("parallel", "parallel"),
        name="gla",
    )(p_main, p_main, p_main, p_main, a_main, p_meta, p_meta, p_meta, p_meta, a_meta,
      w2blk, b2blk, head_g, _level_matrix())


def _conv_rows(y_scr, w, start, rows):
    n = rows + 2 * HALO
    win = y_scr[pl.ds(start - HALO, n), :]
    prev = pltpu.roll(win, 1, 0)[HALO:HALO + rows]
    nxt = pltpu.roll(win, n - 1, 0)[HALO:HALO + rows]
    return prev * w[0:1] + win[HALO:HALO + rows] * w[1:2] + nxt * w[2:3]


def _frame(y_scr, y_meta, seq):
    zeros = jnp.zeros((HALO, y_scr.shape[1]), F32)
    y_scr[0:HALO, :] = zeros
    y_scr[HALO:2 * HALO, :] = y_meta
    y_scr[2 * HALO + seq:3 * HALO + seq, :] = zeros


def _mixconv_body(cb_ref, cc_ref, ch_ref, cbm_ref, ccm_ref, chm_ref, w_ref, o_ref, om_ref, y_scr,
                  *, rows):
    seq = cb_ref.shape[0]
    w = w_ref[...]
    _frame(y_scr, ccm_ref[...].astype(F32) * chm_ref[...].astype(F32), seq)

    def fill(i, carry):
        sl = pl.ds(pl.multiple_of(i * rows, rows), rows)
        dst = pl.ds(pl.multiple_of(2 * HALO + i * rows, HALO), rows)
        y_scr[dst, :] = cc_ref[sl, :].astype(F32) * ch_ref[sl, :].astype(F32)
        return carry
    lax.fori_loop(0, seq // rows, fill, 0)

    om_ref[...] = (cbm_ref[...].astype(F32) * _conv_rows(y_scr, w, HALO, N_META)).astype(BF16)

    def tile(i, carry):
        sl = pl.ds(pl.multiple_of(i * rows, rows), rows)
        start = pl.multiple_of(2 * HALO + i * rows, HALO)
        o_ref[sl, :] = (cb_ref[sl, :].astype(F32) * _conv_rows(y_scr, w, start, rows)).astype(BF16)
        return carry
    lax.fori_loop(0, seq // rows, tile, 0)


def _mixconv(p_main, p_meta, conv_w, *, bsz, seq, shared_meta, rows=256):
    cbo, cco, cho = CB_OFF // CONV_COLS, CC_OFF // CONV_COLS, CH_OFF // CONV_COLS
    mrow = (lambda b: 0) if shared_meta else (lambda b: b)
    main = lambda off: pl.BlockSpec((seq, CONV_COLS), lambda b, j: (b, off + j))
    meta = lambda off: pl.BlockSpec((N_META, CONV_COLS), lambda b, j: (mrow(b), off + j))
    return pl.pallas_call(
        functools.partial(_mixconv_body, rows=rows),
        grid=(bsz, CONV_WIDTH // CONV_COLS),
        in_specs=[main(cbo), main(cco), main(cho), meta(cbo), meta(cco), meta(cho),
                  pl.BlockSpec((3, CONV_COLS), lambda b, j: (0, j))],
        out_specs=[pl.BlockSpec((seq, CONV_COLS), lambda b, j: (b, j)),
                   pl.BlockSpec((N_META, CONV_COLS), lambda b, j: (b, j))],
        out_shape=[jax.ShapeDtypeStruct((bsz * seq, CONV_WIDTH), BF16),
                   jax.ShapeDtypeStruct((bsz * N_META, CONV_WIDTH), BF16)],
        scratch_shapes=[pltpu.VMEM((seq + 3 * HALO, CONV_COLS), F32)],
        compiler_params=_params("parallel", "parallel"),
        name="mixconv",
    )(p_main, p_main, p_main, p_meta, p_meta, p_meta, conv_w)


def _ffn_act_body(u_ref, g_ref, gm_ref, w_ref, b_ref, o_ref, y_scr, *, rows):
    seq = u_ref.shape[0]
    w = w_ref[...]
    bias = b_ref[...]
    _frame(y_scr, gm_ref[...].astype(F32), seq)

    def fill(i, carry):
        sl = pl.ds(pl.multiple_of(i * rows, rows), rows)
        dst = pl.ds(pl.multiple_of(2 * HALO + i * rows, HALO), rows)
        y_scr[dst, :] = g_ref[sl, :].astype(F32)
        return carry
    lax.fori_loop(0, seq // rows, fill, 0)

    def tile(i, carry):
        sl = pl.ds(pl.multiple_of(i * rows, rows), rows)
        start = pl.multiple_of(2 * HALO + i * rows, HALO)
        gate = _silu(_conv_rows(y_scr, w, start, rows) + bias)
        o_ref[sl, :] = (gate * u_ref[sl, :].astype(F32)).astype(BF16)
        return carry
    lax.fori_loop(0, seq // rows, tile, 0)


def _ffn_act(u_main, g_main, g_meta, conv_w, conv_b, *, bsz, seq, rows=256):
    return pl.pallas_call(
        functools.partial(_ffn_act_body, rows=rows),
        grid=(bsz, D_FF // CONV_COLS),
        in_specs=[pl.BlockSpec((seq, CONV_COLS), lambda b, j: (b, j)),
                  pl.BlockSpec((seq, CONV_COLS), lambda b, j: (b, j)),
                  pl.BlockSpec((N_META, CONV_COLS), lambda b, j: (b, j)),
                  pl.BlockSpec((3, CONV_COLS), lambda b, j: (0, j)),
                  pl.BlockSpec((1, CONV_COLS), lambda b, j: (0, j))],
        out_specs=pl.BlockSpec((seq, CONV_COLS), lambda b, j: (b, j)),
        out_shape=jax.ShapeDtypeStruct((bsz * seq, D_FF), BF16),
        scratch_shapes=[pltpu.VMEM((seq + 3 * HALO, CONV_COLS), F32)],
        compiler_params=_params("parallel", "parallel"),
        name="ffn_act",
    )(u_main, g_main, g_meta, conv_w, conv_b)


def _out_proj_body(o_ref, c_ref, wo_ref, wc_ref, x_ref, x1_ref):
    acc = jnp.dot(o_ref[...], wo_ref[...], preferred_element_type=F32)
    acc = acc + jnp.dot(c_ref[...], wc_ref[...], preferred_element_type=F32)
    x1_ref[...] = x_ref[...] + acc


def _out_proj(o, conv, w_out, x, *, tm, tn):
    m = x.shape[0]
    return pl.pallas_call(
        _out_proj_body,
        grid=(m // tm, D_MODEL // tn),
        in_specs=[pl.BlockSpec((tm, GLA_WIDTH), lambda i, j: (i, 0)),
                  pl.BlockSpec((tm, CONV_WIDTH), lambda i, j: (i, 0)),
                  pl.BlockSpec((GLA_WIDTH, tn), lambda i, j: (0, j)),
                  pl.BlockSpec((CONV_WIDTH, tn), lambda i, j: (1, j)),
                  pl.BlockSpec((tm, tn), lambda i, j: (i, j))],
        out_specs=pl.BlockSpec((tm, tn), lambda i, j: (i, j)),
        out_shape=jax.ShapeDtypeStruct((m, D_MODEL), F32),
        compiler_params=_params("parallel", "arbitrary"),
        name="out_proj",
    )(o, conv, w_out, w_out, x)


def _norm_body(x_ref, g_ref, o_ref):
    o_ref[...] = _rmsnorm_rows(x_ref[...], g_ref[...]).astype(o_ref.dtype)


def _norm(x, g, dtype, *, tm):
    m = x.shape[0]
    return pl.pallas_call(
        _norm_body,
        grid=(m // tm,),
        in_specs=[pl.BlockSpec((tm, D_MODEL), lambda i: (i, 0)),
                  pl.BlockSpec((1, D_MODEL), lambda i: (0, 0))],
        out_specs=pl.BlockSpec((tm, D_MODEL), lambda i: (i, 0)),
        out_shape=jax.ShapeDtypeStruct((m, D_MODEL), dtype),
        compiler_params=_params("parallel"),
        name="norm",
    )(x, g)


def _up_proj_body(h_ref, wu_ref, wg_ref, u_ref, g_ref):
    h = h_ref[...]
    u_ref[...] = jnp.dot(h, wu_ref[...], preferred_element_type=F32).astype(BF16)
    g_ref[...] = jnp.dot(h, wg_ref[...], preferred_element_type=F32).astype(BF16)


def _up_proj(h, w_up, *, tm, tn):
    m = h.shape[0]
    nblk = D_FF // tn
    return pl.pallas_call(
        _up_proj_body,
        grid=(m // tm, nblk),
        in_specs=[pl.BlockSpec((tm, D_MODEL), lambda i, j: (i, 0)),
                  pl.BlockSpec((D_MODEL, tn), lambda i, j: (0, j)),
                  pl.BlockSpec((D_MODEL, tn), lambda i, j: (0, nblk + j))],
        out_specs=[pl.BlockSpec((tm, tn), lambda i, j: (i, j)),
                   pl.BlockSpec((tm, tn), lambda i, j: (i, j))],
        out_shape=[jax.ShapeDtypeStruct((m, D_FF), BF16), jax.ShapeDtypeStruct((m, D_FF), BF16)],
        compiler_params=_params("parallel", "arbitrary"),
        name="up_proj",
    )(h, w_up, w_up)


def _down_proj_body(act_ref, w_ref, x1_ref, y_ref):
    y_ref[...] = x1_ref[...] + jnp.dot(act_ref[...], w_ref[...], preferred_element_type=F32)


def _down_proj(act, w_down, x1, *, tm, tn):
    m = act.shape[0]
    return pl.pallas_call(
        _down_proj_body,
        grid=(m // tm, D_MODEL // tn),
        in_specs=[pl.BlockSpec((tm, D_FF), lambda i, j: (i, 0)),
                  pl.BlockSpec((D_FF, tn), lambda i, j: (0, j)),
                  pl.BlockSpec((tm, tn), lambda i, j: (i, j))],
        out_specs=pl.BlockSpec((tm, tn), lambda i, j: (i, j)),
        out_shape=jax.ShapeDtypeStruct((m, D_MODEL), F32),
        compiler_params=_params("parallel", "arbitrary"),
        name="down_proj",
    )(act, w_down, x1)


def _prepare_weights(mix_norm_g, w_in, w_gate2, b_gate2, head_norm_g, conv_mix_w, w_out,
                     ffn_norm_g, w_up, ffn_conv_w, ffn_conv_b, w_down, final_norm_g):
    a_lo = V_OFF + 2 * GLA_WIDTH
    w_main = jnp.concatenate([w_in[:, :a_lo], w_in[:, a_lo + A_COLS:]], axis=1).astype(BF16)
    w_a = w_in[:, a_lo:a_lo + A_COLS].astype(BF16)
    wf = w_gate2[0].reshape(GATE_RANK, GLA_HEADS, GLA_DK).transpose(1, 0, 2)
    wb = w_gate2[1].reshape(GATE_RANK, GLA_HEADS, GLA_DK).transpose(1, 0, 2)
    zero = jnp.zeros_like(wf)
    w2blk = jnp.concatenate([jnp.concatenate([wf, zero], axis=2),
                             jnp.concatenate([zero, wb], axis=2)], axis=1).astype(BF16)
    b2blk = jnp.concatenate([b_gate2[0].reshape(GLA_HEADS, 1, GLA_DK),
                             b_gate2[1].reshape(GLA_HEADS, 1, GLA_DK)], axis=2).astype(F32)
    return dict(
        mix_g=mix_norm_g.reshape(1, D_MODEL), w_main=w_main, w_a=w_a, w2blk=w2blk, b2blk=b2blk,
        head_g=head_norm_g.reshape(1, GLA_DV), conv_mix_w=conv_mix_w, w_out=w_out.astype(BF16),
        ffn_g=ffn_norm_g.reshape(1, D_MODEL), w_up=w_up.astype(BF16), ffn_conv_w=ffn_conv_w,
        ffn_conv_b=ffn_conv_b.reshape(1, D_FF), w_down=w_down.astype(BF16),
        final_g=final_norm_g.reshape(1, D_MODEL))


def _trunk(x, x_meta16, p_meta16, a_meta16, w):
    bsz, seq, _ = x.shape
    x_main = x.reshape(bsz * seq, D_MODEL)
    x_meta = jnp.tile(x_meta16, (bsz, 1))
    m_meta = bsz * N_META

    p_main, a_main = _in_proj(x_main, w["mix_g"], w["w_main"], w["w_a"], tm=512, tn=1024)
    o_main, o_meta = _gla(p_main, a_main, p_meta16, a_meta16, w["w2blk"], w["b2blk"], w["head_g"],
                          bsz=bsz, seq=seq, shared_meta=True)
    c_main, c_meta = _mixconv(p_main, p_meta16, w["conv_mix_w"], bsz=bsz, seq=seq, shared_meta=True)

    x1_main = _out_proj(o_main, c_main, w["w_out"], x_main, tm=1024, tn=512)
    x1_meta = _out_proj(o_meta, c_meta, w["w_out"], x_meta, tm=m_meta, tn=512)
    h2_main = _norm(x1_main, w["ffn_g"], BF16, tm=256)
    h2_meta = _norm(x1_meta, w["ffn_g"], BF16, tm=m_meta)

    u_main, g_main = _up_proj(h2_main, w["w_up"], tm=1024, tn=256)
    _, g_meta = _up_proj(h2_meta, w["w_up"], tm=m_meta, tn=256)
    act = _ffn_act(u_main, g_main, g_meta, w["ffn_conv_w"], w["ffn_conv_b"], bsz=bsz, seq=seq)

    y = _down_proj(act, w["w_down"], x1_main, tm=512, tn=256)
    return _norm(y, w["final_g"], F32, tm=256).reshape(bsz, seq, D_MODEL)


def kernel(x_prompt, x_sample, meta_tokens, mix_norm_g, w_in, w_gate2, b_gate2, head_norm_g,
           conv_mix_w, w_out, ffn_norm_g, w_up, ffn_conv_w, ffn_conv_b, w_down, final_norm_g):
    w = _prepare_weights(mix_norm_g[0], w_in[0], w_gate2[0], b_gate2[0], head_norm_g[0],
                         conv_mix_w[0], w_out[0], ffn_norm_g[0], w_up[0], ffn_conv_w[0],
                         ffn_conv_b[0], w_down[0], final_norm_g)
    p_meta16, a_meta16 = _in_proj(meta_tokens, w["mix_g"], w["w_main"], w["w_a"], tm=N_META, tn=1024)
    y_prompt = _trunk(x_prompt, meta_tokens, p_meta16, a_meta16, w)
    y_sample = _trunk(x_sample, meta_tokens, p_meta16, a_meta16, w)
    return (y_prompt, y_sample)
```

```python
import functools

import numpy as np
import jax
import jax.numpy as jnp
from jax import lax
from jax.experimental import pallas as pl
from jax.experimental.pallas import tpu as pltpu

F32 = jnp.float32
BF16 = jnp.bfloat16

D_MODEL = 4096
N_META = 16
GLA_HEADS = 8
GLA_DK = 128
GLA_DV = 256
GLA_KEY_WIDTH = GLA_HEADS * GLA_DK
GLA_WIDTH = GLA_HEADS * GLA_DV
CONV_WIDTH = 2048
GATE_RANK = 16
GATE_TAU = 16.0
D_FF = 11008
EPS = 1e-6

P_COLS = 2 * GLA_KEY_WIDTH + 2 * GLA_WIDTH + 3 * CONV_WIDTH
A_COLS = 2 * GATE_RANK
Q_OFF, K_OFF, V_OFF, R_OFF = 0, GLA_KEY_WIDTH, 2 * GLA_KEY_WIDTH, 2 * GLA_KEY_WIDTH + GLA_WIDTH
CB_OFF = R_OFF + GLA_WIDTH
CC_OFF = CB_OFF + CONV_WIDTH
CH_OFF = CC_OFF + CONV_WIDTH

CHUNK = 256
LEVELS = 8
HALO = 16
CONV_COLS = 256
V7X_VMEM_LIMIT = 56 * 1024 * 1024


def _params(*sem):
    return pltpu.CompilerParams(dimension_semantics=sem, vmem_limit_bytes=V7X_VMEM_LIMIT)


def _rmsnorm_rows(x, g):
    ms = jnp.mean(x * x, axis=-1, keepdims=True)
    return x * lax.rsqrt(ms + EPS) * g


def _silu(x):
    return x * (1.0 / (1.0 + jnp.exp(-x)))


def _in_proj_body(x_ref, g_ref, w_ref, wa_ref, p_ref, a_ref, h_scr, *, rows):
    @pl.when(pl.program_id(1) == 0)
    def _():
        def norm_rows(i, carry):
            sl = pl.ds(pl.multiple_of(i * rows, rows), rows)
            h_scr[sl, :] = _rmsnorm_rows(x_ref[sl, :], g_ref[...]).astype(BF16)
            return carry
        lax.fori_loop(0, x_ref.shape[0] // rows, norm_rows, 0)
        a_ref[...] = jnp.dot(h_scr[...], wa_ref[...], preferred_element_type=F32)

    p_ref[...] = jnp.dot(h_scr[...], w_ref[...], preferred_element_type=F32).astype(BF16)


def _in_proj(x, g, w_main, w_a, *, tm, tn):
    m = x.shape[0]
    return pl.pallas_call(
        functools.partial(_in_proj_body, rows=min(32, tm)),
        grid=(m // tm, P_COLS // tn),
        in_specs=[pl.BlockSpec((tm, D_MODEL), lambda i, j: (i, 0)),
                  pl.BlockSpec((1, D_MODEL), lambda i, j: (0, 0)),
                  pl.BlockSpec((D_MODEL, tn), lambda i, j: (0, j)),
                  pl.BlockSpec((D_MODEL, A_COLS), lambda i, j: (0, 0))],
        out_specs=[pl.BlockSpec((tm, tn), lambda i, j: (i, j)),
                   pl.BlockSpec((tm, A_COLS), lambda i, j: (i, 0))],
        out_shape=[jax.ShapeDtypeStruct((m, P_COLS), BF16),
                   jax.ShapeDtypeStruct((m, A_COLS), F32)],
        scratch_shapes=[pltpu.VMEM((tm, D_MODEL), BF16)],
        compiler_params=_params("parallel", "arbitrary"),
        name="in_proj",
    )(x, g, w_main, w_a)


def _log_decay(a, w2, b2):
    z = jnp.dot(a.astype(BF16), w2, preferred_element_type=F32) + b2
    return (jnp.minimum(z, 0.0) - jnp.log1p(jnp.exp(-jnp.abs(z)))) * (1.0 / GATE_TAU)


def _chunk_intra(q, k, v, lg, lv):
    shape = (CHUNK, 2 * GLA_DK)
    row = lax.broadcasted_iota(jnp.int32, shape, 0)
    isb = lax.broadcasted_iota(jnp.int32, shape, 1) >= GLA_DK
    q2 = jnp.concatenate([q, q], axis=1)
    k2 = jnp.concatenate([k, k], axis=1)
    p = jnp.where(isb, 0.0, lg)
    tb = lg
    scores = jnp.zeros((CHUNK, CHUNK), F32)
    for lvl in range(LEVELS):
        m = 1 << lvl
        upper = (row & m) != 0
        e = jnp.exp(jnp.where(upper, p, tb - p))
        qside = jnp.logical_xor(upper, isb)
        ql = jnp.where(qside, q2 * e, 0.0).astype(BF16)
        kl = jnp.where(qside, 0.0, k2 * e).astype(BF16)
        s = lax.dot_general(ql, kl, (((1,), (1,)), ((), ())), preferred_element_type=F32)
        scores = jnp.where(lv == lvl, s, scores)
        tb_prev = pltpu.roll(tb, m, 0)
        tb_next = pltpu.roll(tb, CHUNK - m, 0)
        p = p + jnp.where(upper, tb_prev, 0.0)
        tb = tb + jnp.where(upper, tb_prev, tb_next)
    vb = v.astype(BF16)
    qk = jnp.sum(q * k, axis=-1, keepdims=True)
    o_part = jnp.dot(scores.astype(BF16), vb, preferred_element_type=F32) + qk * v
    khat = (k2 * jnp.exp(jnp.where(isb, p, tb - p))).astype(BF16)
    qhat = (q2 * jnp.exp(jnp.where(isb, tb - p, p))).astype(BF16)
    dstate = lax.dot_general(vb, khat, (((0,), (0,)), ((), ())), preferred_element_type=F32)
    return o_part, qhat, dstate, jnp.exp(tb[0:1, :])


def _gla_body(q_ref, k_ref, v_ref, r_ref, a_ref, qm_ref, km_ref, vm_ref, rm_ref, am_ref,
              w2_ref, b2_ref, hg_ref, lv_ref, o_ref, om_ref,
              opart_scr, qhat_scr, st_scr, dec_scr):
    n_main = q_ref.shape[0] // CHUNK
    n = n_main + 1
    scale = GLA_DK ** -0.5
    w2 = w2_ref[...]
    b2 = b2_ref[...]
    pad = CHUNK - N_META

    def intra(c, q, k, v, lg):
        o_part, qhat, dstate, dec = _chunk_intra(q * scale, k, v, lg, lv_ref[...])
        opart_scr[c] = o_part
        qhat_scr[c] = qhat
        st_scr[c] = dstate
        dec_scr[c] = dec

    def behind_zeros(x):
        return jnp.concatenate([jnp.zeros((pad, x.shape[1]), F32), x.astype(F32)], axis=0)

    intra(0, behind_zeros(qm_ref[...]), behind_zeros(km_ref[...]), behind_zeros(vm_ref[...]),
          behind_zeros(_log_decay(am_ref[...], w2, b2)))

    def intra_main(c, carry):
        sl = pl.ds(pl.multiple_of(c * CHUNK, CHUNK), CHUNK)
        intra(c + 1, q_ref[sl, :].astype(F32), k_ref[sl, :].astype(F32), v_ref[sl, :].astype(F32),
              _log_decay(a_ref[sl, :], w2, b2))
        return carry
    lax.fori_loop(0, n_main, intra_main, 0)

    def scan_fwd(c, s):
        d = st_scr[c, :, 0:GLA_DK]
        st_scr[c, :, 0:GLA_DK] = s
        return dec_scr[c][:, 0:GLA_DK] * s + d
    lax.fori_loop(0, n, scan_fwd, jnp.zeros((GLA_DV, GLA_DK), F32))

    def scan_bwd(i, s):
        c = n - 1 - i
        d = st_scr[c, :, GLA_DK:]
        st_scr[c, :, GLA_DK:] = s
        return dec_scr[c][:, GLA_DK:] * s + d
    lax.fori_loop(0, n, scan_bwd, jnp.zeros((GLA_DV, GLA_DK), F32))

    def normed(c):
        o = opart_scr[c] + lax.dot_general(qhat_scr[c], st_scr[c].astype(BF16),
                                           (((1,), (1,)), ((), ())), preferred_element_type=F32)
        return _rmsnorm_rows(o, hg_ref[...])

    om_ref[...] = (normed(0)[pad:, :] * _silu(rm_ref[...].astype(F32))).astype(BF16)

    def finish_main(c, carry):
        sl = pl.ds(pl.multiple_of(c * CHUNK, CHUNK), CHUNK)
        o_ref[sl, :] = (normed(c + 1) * _silu(r_ref[sl, :].astype(F32))).astype(BF16)
        return carry
    lax.fori_loop(0, n_main, finish_main, 0)


def _level_matrix():
    t = np.arange(CHUNK)
    x = t[:, None] ^ t[None, :]
    lv = np.full((CHUNK, CHUNK), -1, np.int32)
    nz = x > 0
    lv[nz] = np.floor(np.log2(x[nz])).astype(np.int32)
    return jnp.asarray(lv)


def _gla(p_main, a_main, p_meta, a_meta, w2blk, b2blk, head_g, *, bsz, seq, shared_meta):
    n = seq // CHUNK + 1
    qo, ko = Q_OFF // GLA_DK, K_OFF // GLA_DK
    vo, ro = V_OFF // GLA_DV, R_OFF // GLA_DV
    mrow = (lambda b: 0) if shared_meta else (lambda b: b)
    main = lambda width, off: pl.BlockSpec((seq, width), lambda b, h: (b, off + h))
    meta = lambda width, off: pl.BlockSpec((N_META, width), lambda b, h: (mrow(b), off + h))
    return pl.pallas_call(
        _gla_body,
        grid=(bsz, GLA_HEADS),
        in_specs=[main(GLA_DK, qo), main(GLA_DK, ko), main(GLA_DV, vo), main(GLA_DV, ro),
                  pl.BlockSpec((seq, A_COLS), lambda b, h: (b, 0)),
                  meta(GLA_DK, qo), meta(GLA_DK, ko), meta(GLA_DV, vo), meta(GLA_DV, ro),
                  pl.BlockSpec((N_META, A_COLS), lambda b, h: (mrow(b), 0)),
                  pl.BlockSpec((None, A_COLS, 2 * GLA_DK), lambda b, h: (h, 0, 0)),
                  pl.BlockSpec((None, 1, 2 * GLA_DK), lambda b, h: (h, 0, 0)),
                  pl.BlockSpec((1, GLA_DV), lambda b, h: (0, 0)),
                  pl.BlockSpec((CHUNK, CHUNK), lambda b, h: (0, 0))],
        out_specs=[pl.BlockSpec((seq, GLA_DV), lambda b, h: (b, h)),
                   pl.BlockSpec((N_META, GLA_DV), lambda b, h: (b, h))],
        out_shape=[jax.ShapeDtypeStruct((bsz * seq, GLA_WIDTH), BF16),
                   jax.ShapeDtypeStruct((bsz * N_META, GLA_WIDTH), BF16)],
        scratch_shapes=[pltpu.VMEM((n, CHUNK, GLA_DV), F32),
                        pltpu.VMEM((n, CHUNK, 2 * GLA_DK), BF16),
                        pltpu.VMEM((n, GLA_DV, 2 * GLA_DK), F32),
                        pltpu.VMEM((n, 1, 2 * GLA_DK), F32)],
        compiler_params=_params("parallel", "parallel"),
        name="gla",
    )(p_main, p_main, p_main, p_main, a_main, p_meta, p_meta, p_meta, p_meta, a_meta,
      w2blk, b2blk, head_g, _level_matrix())


def _conv_rows(y_scr, w, start, rows):
    n = rows + 2 * HALO
    win = y_scr[pl.ds(start - HALO, n), :]
    prev = pltpu.roll(win, 1, 0)[HALO:HALO + rows]
    nxt = pltpu.roll(win, n - 1, 0)[HALO:HALO + rows]
    return prev * w[0:1] + win[HALO:HALO + rows] * w[1:2] + nxt * w[2:3]


def _frame(y_scr, y_meta, seq):
    zeros = jnp.zeros((HALO, y_scr.shape[1]), F32)
    y_scr[0:HALO, :] = zeros
    y_scr[HALO:2 * HALO, :] = y_meta
    y_scr[2 * HALO + seq:3 * HALO + seq, :] = zeros


def _mixconv_body(cb_ref, cc_ref, ch_ref, cbm_ref, ccm_ref, chm_ref, w_ref, o_ref, om_ref, y_scr,
                  *, rows):
    seq = cb_ref.shape[0]
    w = w_ref[...]
    _frame(y_scr, ccm_ref[...].astype(F32) * chm_ref[...].astype(F32), seq)

    def fill(i, carry):
        sl = pl.ds(pl.multiple_of(i * rows, rows), rows)
        dst = pl.ds(pl.multiple_of(2 * HALO + i * rows, HALO), rows)
        y_scr[dst, :] = cc_ref[sl, :].astype(F32) * ch_ref[sl, :].astype(F32)
        return carry
    lax.fori_loop(0, seq // rows, fill, 0)

    om_ref[...] = (cbm_ref[...].astype(F32) * _conv_rows(y_scr, w, HALO, N_META)).astype(BF16)

    def tile(i, carry):
        sl = pl.ds(pl.multiple_of(i * rows, rows), rows)
        start = pl.multiple_of(2 * HALO + i * rows, HALO)
        o_ref[sl, :] = (cb_ref[sl, :].astype(F32) * _conv_rows(y_scr, w, start, rows)).astype(BF16)
        return carry
    lax.fori_loop(0, seq // rows, tile, 0)


def _mixconv(p_main, p_meta, conv_w, *, bsz, seq, shared_meta, rows=256):
    cbo, cco, cho = CB_OFF // CONV_COLS, CC_OFF // CONV_COLS, CH_OFF // CONV_COLS
    mrow = (lambda b: 0) if shared_meta else (lambda b: b)
    main = lambda off: pl.BlockSpec((seq, CONV_COLS), lambda b, j: (b, off + j))
    meta = lambda off: pl.BlockSpec((N_META, CONV_COLS), lambda b, j: (mrow(b), off + j))
    return pl.pallas_call(
        functools.partial(_mixconv_body, rows=rows),
        grid=(bsz, CONV_WIDTH // CONV_COLS),
        in_specs=[main(cbo), main(cco), main(cho), meta(cbo), meta(cco), meta(cho),
                  pl.BlockSpec((3, CONV_COLS), lambda b, j: (0, j))],
        out_specs=[pl.BlockSpec((seq, CONV_COLS), lambda b, j: (b, j)),
                   pl.BlockSpec((N_META, CONV_COLS), lambda b, j: (b, j))],
        out_shape=[jax.ShapeDtypeStruct((bsz * seq, CONV_WIDTH), BF16),
                   jax.ShapeDtypeStruct((bsz * N_META, CONV_WIDTH), BF16)],
        scratch_shapes=[pltpu.VMEM((seq + 3 * HALO, CONV_COLS), F32)],
        compiler_params=_params("parallel", "parallel"),
        name="mixconv",
    )(p_main, p_main, p_main, p_meta, p_meta, p_meta, conv_w)


def _norm_block_rows(src_ref, dst_ref, g_ref, rows):
    def body(i, carry):
        sl = pl.ds(pl.multiple_of(i * rows, rows), rows)
        dst_ref[sl, :] = _rmsnorm_rows(src_ref[sl, :], g_ref[...]).astype(dst_ref.dtype)
        return carry
    lax.fori_loop(0, src_ref.shape[0] // rows, body, 0)


def _out_proj_body(o_ref, c_ref, wo_ref, wc_ref, x_ref, g_ref, x1_ref, h2_ref, *, tn, rows):
    j = pl.program_id(1)
    acc = jnp.dot(o_ref[...], wo_ref[...], preferred_element_type=F32)
    acc = acc + jnp.dot(c_ref[...], wc_ref[...], preferred_element_type=F32)
    x1_ref[:, pl.ds(pl.multiple_of(j * tn, tn), tn)] = x_ref[...] + acc

    @pl.when(j == pl.num_programs(1) - 1)
    def _():
        _norm_block_rows(x1_ref, h2_ref, g_ref, rows)


def _out_proj(o, conv, w_out, x, g, *, tm, tn):
    m = x.shape[0]
    return pl.pallas_call(
        functools.partial(_out_proj_body, tn=tn, rows=min(32, tm)),
        grid=(m // tm, D_MODEL // tn),
        in_specs=[pl.BlockSpec((tm, GLA_WIDTH), lambda i, j: (i, 0)),
                  pl.BlockSpec((tm, CONV_WIDTH), lambda i, j: (i, 0)),
                  pl.BlockSpec((GLA_WIDTH, tn), lambda i, j: (0, j)),
                  pl.BlockSpec((CONV_WIDTH, tn), lambda i, j: (1, j)),
                  pl.BlockSpec((tm, tn), lambda i, j: (i, j)),
                  pl.BlockSpec((1, D_MODEL), lambda i, j: (0, 0))],
        out_specs=[pl.BlockSpec((tm, D_MODEL), lambda i, j: (i, 0)),
                   pl.BlockSpec((tm, D_MODEL), lambda i, j: (i, 0))],
        out_shape=[jax.ShapeDtypeStruct((m, D_MODEL), F32),
                   jax.ShapeDtypeStruct((m, D_MODEL), BF16)],
        compiler_params=_params("parallel", "arbitrary"),
        name="out_proj",
    )(o, conv, w_out, w_out, x, g)


def _up_proj_body(h_ref, hp_ref, hn_ref, hm_ref, wu_ref, wg_ref, cw_ref, cb_ref, act_ref, lhs_scr,
                  *, tiles_per_seq, rows):
    tm = h_ref.shape[0]
    i = pl.program_id(0)

    @pl.when(pl.program_id(1) == 0)
    def _():
        first = (i % tiles_per_seq) == 0
        last = (i % tiles_per_seq) == tiles_per_seq - 1

        @pl.when(first)
        def _():
            lhs_scr[0:HALO, :] = hm_ref[...]

        @pl.when(jnp.logical_not(first))
        def _():
            lhs_scr[0:HALO, :] = hp_ref[...]

        @pl.when(last)
        def _():
            lhs_scr[HALO + tm:2 * HALO + tm, :] = jnp.zeros((HALO, D_MODEL), BF16)

        @pl.when(jnp.logical_not(last))
        def _():
            lhs_scr[HALO + tm:2 * HALO + tm, :] = hn_ref[...]

        def copy_rows(r, carry):
            sl = pl.ds(pl.multiple_of(r * rows, rows), rows)
            lhs_scr[pl.ds(pl.multiple_of(HALO + r * rows, HALO), rows), :] = h_ref[sl, :]
            return carry
        lax.fori_loop(0, tm // rows, copy_rows, 0)

    n = tm + 2 * HALO
    g = jnp.dot(lhs_scr[...], wg_ref[...], preferred_element_type=F32)
    u = jnp.dot(h_ref[...], wu_ref[...], preferred_element_type=F32)
    w = cw_ref[...]
    conv = (pltpu.roll(g, 1, 0)[HALO:HALO + tm] * w[0:1] + g[HALO:HALO + tm] * w[1:2]
            + pltpu.roll(g, n - 1, 0)[HALO:HALO + tm] * w[2:3])
    act_ref[...] = (_silu(conv + cb_ref[...]) * u).astype(BF16)


def _up_proj(h_main, h_meta, w_up, conv_w, conv_b, *, seq, meta_off, tm, tn):
    m = h_main.shape[0]
    nblk = D_FF // tn
    tiles_per_seq = seq // tm
    halo_per_tile = tm // HALO
    n_halo = m // HALO
    return pl.pallas_call(
        functools.partial(_up_proj_body, tiles_per_seq=tiles_per_seq, rows=64),
        grid=(m // tm, nblk),
        in_specs=[pl.BlockSpec((tm, D_MODEL), lambda i, j: (i, 0)),
                  pl.BlockSpec((HALO, D_MODEL),
                               lambda i, j: (jnp.maximum(i * halo_per_tile - 1, 0), 0)),
                  pl.BlockSpec((HALO, D_MODEL),
                               lambda i, j: (jnp.minimum((i + 1) * halo_per_tile, n_halo - 1), 0)),
                  pl.BlockSpec((HALO, D_MODEL), lambda i, j: (meta_off + i // tiles_per_seq, 0)),
                  pl.BlockSpec((D_MODEL, tn), lambda i, j: (0, j)),
                  pl.BlockSpec((D_MODEL, tn), lambda i, j: (0, nblk + j)),
                  pl.BlockSpec((3, tn), lambda i, j: (0, j)),
                  pl.BlockSpec((1, tn), lambda i, j: (0, j))],
        out_specs=pl.BlockSpec((tm, tn), lambda i, j: (i, j)),
        out_shape=jax.ShapeDtypeStruct((m, D_FF), BF16),
        scratch_shapes=[pltpu.VMEM((tm + 2 * HALO, D_MODEL), BF16)],
        compiler_params=_params("parallel", "arbitrary"),
        name="up_proj",
    )(h_main, h_main, h_main, h_meta, w_up, w_up, conv_w, conv_b)


def _down_proj_body(act_ref, w_ref, x1_ref, g_ref, y_ref, *, tn, rows):
    j = pl.program_id(1)
    y_ref[:, pl.ds(pl.multiple_of(j * tn, tn), tn)] = (
        x1_ref[...] + jnp.dot(act_ref[...], w_ref[...], preferred_element_type=F32))

    @pl.when(j == pl.num_programs(1) - 1)
    def _():
        _norm_block_rows(y_ref, y_ref, g_ref, rows)


def _down_proj(act, w_down, x1, g, *, tm, tn):
    m = act.shape[0]
    return pl.pallas_call(
        functools.partial(_down_proj_body, tn=tn, rows=32),
        grid=(m // tm, D_MODEL // tn),
        in_specs=[pl.BlockSpec((tm, D_FF), lambda i, j: (i, 0)),
                  pl.BlockSpec((D_FF, tn), lambda i, j: (0, j)),
                  pl.BlockSpec((tm, tn), lambda i, j: (i, j)),
                  pl.BlockSpec((1, D_MODEL), lambda i, j: (0, 0))],
        out_specs=pl.BlockSpec((tm, D_MODEL), lambda i, j: (i, 0)),
        out_shape=jax.ShapeDtypeStruct((m, D_MODEL), F32),
        compiler_params=_params("parallel", "arbitrary"),
        name="down_proj",
    )(act, w_down, x1, g)


def _prepare_weights(mix_norm_g, w_in, w_gate2, b_gate2, head_norm_g, conv_mix_w, w_out,
                     ffn_norm_g, w_up, ffn_conv_w, ffn_conv_b, w_down, final_norm_g):
    a_lo = V_OFF + 2 * GLA_WIDTH
    w_main = jnp.concatenate([w_in[:, :a_lo].astype(BF16), w_in[:, a_lo + A_COLS:].astype(BF16)],
                             axis=1)
    w_a = w_in[:, a_lo:a_lo + A_COLS].astype(BF16)
    wf = w_gate2[0].reshape(GATE_RANK, GLA_HEADS, GLA_DK).transpose(1, 0, 2)
    wb = w_gate2[1].reshape(GATE_RANK, GLA_HEADS, GLA_DK).transpose(1, 0, 2)
    zero = jnp.zeros_like(wf)
    w2blk = jnp.concatenate([jnp.concatenate([wf, zero], axis=2),
                             jnp.concatenate([zero, wb], axis=2)], axis=1).astype(BF16)
    b2blk = jnp.concatenate([b_gate2[0].reshape(GLA_HEADS, 1, GLA_DK),
                             b_gate2[1].reshape(GLA_HEADS, 1, GLA_DK)], axis=2).astype(F32)
    return dict(
        mix_g=mix_norm_g.reshape(1, D_MODEL), w_main=w_main, w_a=w_a, w2blk=w2blk, b2blk=b2blk,
        head_g=head_norm_g.reshape(1, GLA_DV), conv_mix_w=conv_mix_w, w_out=w_out.astype(BF16),
        ffn_g=ffn_norm_g.reshape(1, D_MODEL), w_up=w_up.astype(BF16), ffn_conv_w=ffn_conv_w,
        ffn_conv_b=ffn_conv_b.reshape(1, D_FF), w_down=w_down.astype(BF16),
        final_g=final_norm_g.reshape(1, D_MODEL))


def kernel(x_prompt, x_sample, meta_tokens, mix_norm_g, w_in, w_gate2, b_gate2, head_norm_g,
           conv_mix_w, w_out, ffn_norm_g, w_up, ffn_conv_w, ffn_conv_b, w_down, final_norm_g):
    w = _prepare_weights(mix_norm_g[0], w_in[0], w_gate2[0], b_gate2[0], head_norm_g[0],
                         conv_mix_w[0], w_out[0], ffn_norm_g[0], w_up[0], ffn_conv_w[0],
                         ffn_conv_b[0], w_down[0], final_norm_g)
    trunks = (x_prompt, x_sample)
    p_meta16, a_meta16 = _in_proj(meta_tokens, w["mix_g"], w["w_main"], w["w_a"], tm=N_META, tn=1024)

    mixed = []
    for x in trunks:
        bsz, seq, _ = x.shape
        x_main = x.reshape(bsz * seq, D_MODEL)
        p_main, a_main = _in_proj(x_main, w["mix_g"], w["w_main"], w["w_a"], tm=512, tn=1024)
        o_main, o_meta = _gla(p_main, a_main, p_meta16, a_meta16, w["w2blk"], w["b2blk"],
                              w["head_g"], bsz=bsz, seq=seq, shared_meta=True)
        c_main, c_meta = _mixconv(p_main, p_meta16, w["conv_mix_w"], bsz=bsz, seq=seq,
                                  shared_meta=True)
        mixed.append((x_main, o_main, c_main, o_meta, c_meta))

    n_seq = sum(x.shape[0] for x in trunks)
    _, h2_meta = _out_proj(jnp.concatenate([t[3] for t in mixed], axis=0),
                           jnp.concatenate([t[4] for t in mixed], axis=0),
                           w["w_out"], jnp.tile(meta_tokens, (n_seq, 1)), w["ffn_g"],
                           tm=n_seq * N_META, tn=512)

    outs = []
    meta_off = 0
    for x, (x_main, o_main, c_main, _, _) in zip(trunks, mixed):
        bsz, seq, _ = x.shape
        x1, h2 = _out_proj(o_main, c_main, w["w_out"], x_main, w["ffn_g"], tm=512, tn=512)
        act = _up_proj(h2, h2_meta, w["w_up"], w["ffn_conv_w"], w["ffn_conv_b"],
                       seq=seq, meta_off=meta_off, tm=1024, tn=256)
        y = _down_proj(act, w["w_down"], x1, w["final_g"], tm=512, tn=256)
        outs.append(y.reshape(bsz, seq, D_MODEL))
        meta_off += bsz
    return tuple(outs)
```

```python
import functools

import numpy as np
import jax
import jax.numpy as jnp
from jax import lax
from jax.experimental import pallas as pl
from jax.experimental.pallas import tpu as pltpu

F32 = jnp.float32
BF16 = jnp.bfloat16

D_MODEL = 4096
N_META = 16
GLA_HEADS = 8
GLA_DK = 128
GLA_DV = 256
GLA_KEY_WIDTH = GLA_HEADS * GLA_DK
GLA_WIDTH = GLA_HEADS * GLA_DV
CONV_WIDTH = 2048
GATE_RANK = 16
GATE_TAU = 16.0
D_FF = 11008
EPS = 1e-6

P_COLS = 2 * GLA_KEY_WIDTH + 2 * GLA_WIDTH + 3 * CONV_WIDTH
A_COLS = 2 * GATE_RANK
Q_OFF, K_OFF, V_OFF, R_OFF = 0, GLA_KEY_WIDTH, 2 * GLA_KEY_WIDTH, 2 * GLA_KEY_WIDTH + GLA_WIDTH
CB_OFF = R_OFF + GLA_WIDTH
CC_OFF = CB_OFF + CONV_WIDTH
CH_OFF = CC_OFF + CONV_WIDTH

CHUNK = 256
LEVELS = 8
F32_TILE_ROWS = 8
BF16_TILE_ROWS = 16
TILE_LEVELS = 3
LOG2_E = 1.4426950408889634
HALO = 16
CONV_COLS = 256
V7X_VMEM_LIMIT = 56 * 1024 * 1024
IN_PROJ_TN = 1024
OUT_PROJ_TN = 512
UP_PROJ_TN = 256
DOWN_PROJ_TN = 256


def _params(*sem):
    return pltpu.CompilerParams(dimension_semantics=sem, vmem_limit_bytes=V7X_VMEM_LIMIT)


def _rmsnorm_rows(x, g):
    ms = jnp.mean(x * x, axis=-1, keepdims=True)
    return x * lax.rsqrt(ms + EPS) * g


def _silu(x):
    return x * (1.0 / (1.0 + jnp.exp(-x)))


def _in_proj_body(x_ref, g_ref, w_ref, wa_ref, p_ref, a_ref, h_scr, *, rows):
    @pl.when(pl.program_id(1) == 0)
    def _():
        def norm_rows(i, carry):
            sl = pl.ds(pl.multiple_of(i * rows, rows), rows)
            h_scr[sl, :] = _rmsnorm_rows(x_ref[sl, :], g_ref[...]).astype(BF16)
            return carry
        lax.fori_loop(0, x_ref.shape[0] // rows, norm_rows, 0)
        a_ref[...] = jnp.dot(h_scr[...], wa_ref[...], preferred_element_type=F32)

    p_ref[...] = jnp.dot(h_scr[...], w_ref[...], preferred_element_type=F32).astype(BF16)


def _in_proj(x, g, w_main, w_a, *, tm):
    m = x.shape[0]
    tn = w_main.shape[2]
    return pl.pallas_call(
        functools.partial(_in_proj_body, rows=min(32, tm)),
        grid=(m // tm, P_COLS // tn),
        in_specs=[pl.BlockSpec((tm, D_MODEL), lambda i, j: (i, 0)),
                  pl.BlockSpec((1, D_MODEL), lambda i, j: (0, 0)),
                  pl.BlockSpec((None, D_MODEL, tn), lambda i, j: (j, 0, 0)),
                  pl.BlockSpec((D_MODEL, A_COLS), lambda i, j: (0, 0))],
        out_specs=[pl.BlockSpec((tm, tn), lambda i, j: (i, j)),
                   pl.BlockSpec((tm, A_COLS), lambda i, j: (i, 0))],
        out_shape=[jax.ShapeDtypeStruct((m, P_COLS), BF16),
                   jax.ShapeDtypeStruct((m, A_COLS), F32)],
        scratch_shapes=[pltpu.VMEM((tm, D_MODEL), BF16)],
        compiler_params=_params("parallel", "arbitrary"),
        name="in_proj",
    )(x, g, w_main, w_a)


def _log2_decay(a, w2, b2):
    z = jnp.dot(a.astype(BF16), w2, preferred_element_type=F32) + b2
    return (jnp.minimum(z, 0.0) - jnp.log(1.0 + jnp.exp(-jnp.abs(z)))) * (LOG2_E / GATE_TAU)


def _nt_dot(a, b):
    return lax.dot_general(a, b, (((1,), (1,)), ((), ())), preferred_element_type=F32)


def _chunk_intra(q, k, v, lg, lvd):
    half = CHUNK // 2
    shape = (CHUNK, 2 * GLA_DK)
    row = lax.broadcasted_iota(jnp.int32, shape, 0)
    isb = lax.broadcasted_iota(jnp.int32, shape, 1) >= GLA_DK
    qb = q.astype(BF16)
    q2b = jnp.concatenate([qb, qb], axis=1)
    kb = k.astype(BF16)
    k2b = jnp.concatenate([kb, kb], axis=1)
    zero = jnp.zeros(shape, BF16)
    diag = [jnp.zeros((half, half), F32), jnp.zeros((half, half), F32)]

    def keep_level(lvl, ql, kl):
        for d in range(2):
            rows = slice(d * half, (d + 1) * half)
            diag[d] = jnp.where(lvd == lvl, _nt_dot(ql[rows], kl[rows]), diag[d])

    def masked_level(lvl, e):
        upper = (row & (1 << lvl)) != 0
        qside = jnp.logical_xor(upper, isb)
        keep_level(lvl, jnp.where(qside, q2b * e, zero), k2b * e)
        return upper

    n_tiles = CHUNK // F32_TILE_ROWS
    tiled = (n_tiles, F32_TILE_ROWS, 2 * GLA_DK)
    row_t = lax.broadcasted_iota(jnp.int32, (1,) + tiled[1:], 1)
    isb_t = lax.broadcasted_iota(jnp.int32, (1,) + tiled[1:], 2) >= GLA_DK
    tb = lg.reshape(tiled)
    p = jnp.where(isb_t, 0.0, tb)
    for lvl in range(TILE_LEVELS):
        m = 1 << lvl
        upper_t = (row_t & m) != 0
        e = jnp.exp2(jnp.where(upper_t, p, tb - p)).reshape(shape).astype(BF16)
        masked_level(lvl, e)
        tb_prev = pltpu.roll(tb, m, 1)
        tb_next = pltpu.roll(tb, F32_TILE_ROWS - m, 1)
        p = p + jnp.where(upper_t, tb_prev, 0.0)
        tb = tb + jnp.where(upper_t, tb_prev, tb_next)

    off = [jnp.zeros(tiled[1:], F32)]
    for j in range(n_tiles - 1):
        off.append(off[-1] + tb[j])
    total = off[-1] + tb[n_tiles - 1]
    pre_t = [p[j] + off[j] for j in range(n_tiles)]

    for lvl in range(TILE_LEVELS, LEVELS):
        m = 1 << lvl
        tpb = m // F32_TILE_ROWS
        args = []
        for j in range(n_tiles):
            later_start = off[(j // (2 * tpb) * 2 + 1) * tpb]
            args.append(pre_t[j] - later_start if (j // tpb) % 2 else later_start - pre_t[j])
        e = jnp.exp2(jnp.concatenate(args, axis=0)).astype(BF16)
        if m < BF16_TILE_ROWS:
            masked_level(lvl, e)
        elif m < half:
            zeros_m = jnp.zeros((m, GLA_DK), BF16)
            qf, qr = [], []
            for g in range(CHUNK // (2 * m)):
                lo = slice(2 * m * g, 2 * m * g + m)
                up = slice(2 * m * g + m, 2 * m * (g + 1))
                qf += [zeros_m, qb[up] * e[up, :GLA_DK]]
                qr += [qb[lo] * e[lo, GLA_DK:], zeros_m]
            ql = jnp.concatenate([jnp.concatenate(qf, axis=0), jnp.concatenate(qr, axis=0)], axis=1)
            keep_level(lvl, ql, k2b * e)
        else:
            s_lower = _nt_dot(qb[half:] * e[half:, :GLA_DK], kb[:half] * e[:half, :GLA_DK])
            s_upper = _nt_dot(qb[:half] * e[:half, GLA_DK:], kb[half:] * e[half:, GLA_DK:])
    scores = jnp.concatenate([jnp.concatenate([diag[0], s_upper], axis=1),
                              jnp.concatenate([s_lower, diag[1]], axis=1)], axis=0)

    vb = v.astype(BF16)
    qk = jnp.sum(q * k, axis=-1, keepdims=True)
    o_part = jnp.dot(scores.astype(BF16), vb, preferred_element_type=F32) + qk * v
    pre = jnp.concatenate(pre_t, axis=0)
    rest = jnp.concatenate([total] * n_tiles, axis=0) - pre
    khat = k2b * jnp.exp2(jnp.concatenate([rest[:, :GLA_DK], pre[:, GLA_DK:]], axis=1)).astype(BF16)
    qhat = q2b * jnp.exp2(jnp.concatenate([pre[:, :GLA_DK], rest[:, GLA_DK:]], axis=1)).astype(BF16)
    dstate = lax.dot_general(vb, khat, (((0,), (0,)), ((), ())), preferred_element_type=F32)
    return o_part, qhat, dstate, jnp.exp2(total[0:1, :])


def _gla_body(q_ref, k_ref, v_ref, r_ref, a_ref, qm_ref, km_ref, vm_ref, rm_ref, am_ref,
              w2_ref, b2_ref, hg_ref, lv_ref, o_ref, om_ref,
              opart_scr, qhat_scr, st_scr, dec_scr):
    n_main = q_ref.shape[0] // CHUNK
    n = n_main + 1
    scale = GLA_DK ** -0.5
    w2 = w2_ref[...]
    b2 = b2_ref[...]
    pad = CHUNK - N_META

    def intra(c, q, k, v, lg):
        o_part, qhat, dstate, dec = _chunk_intra(q * scale, k, v, lg, lv_ref[...])
        opart_scr[c] = o_part
        qhat_scr[c] = qhat
        st_scr[c] = dstate
        dec_scr[c] = dec

    def behind_zeros(x):
        return jnp.concatenate([jnp.zeros((pad, x.shape[1]), F32), x.astype(F32)], axis=0)

    intra(0, behind_zeros(qm_ref[...]), behind_zeros(km_ref[...]), behind_zeros(vm_ref[...]),
          behind_zeros(_log2_decay(am_ref[...], w2, b2)))

    def intra_main(c, carry):
        sl = pl.ds(pl.multiple_of(c * CHUNK, CHUNK), CHUNK)
        intra(c + 1, q_ref[sl, :].astype(F32), k_ref[sl, :].astype(F32), v_ref[sl, :].astype(F32),
              _log2_decay(a_ref[sl, :], w2, b2))
        return carry
    lax.fori_loop(0, n_main, intra_main, 0)

    def scan_fwd(c, s):
        d = st_scr[c, :, 0:GLA_DK]
        st_scr[c, :, 0:GLA_DK] = s
        return dec_scr[c][:, 0:GLA_DK] * s + d
    lax.fori_loop(0, n, scan_fwd, jnp.zeros((GLA_DV, GLA_DK), F32))

    def scan_bwd(i, s):
        c = n - 1 - i
        d = st_scr[c, :, GLA_DK:]
        st_scr[c, :, GLA_DK:] = s
        return dec_scr[c][:, GLA_DK:] * s + d
    lax.fori_loop(0, n, scan_bwd, jnp.zeros((GLA_DV, GLA_DK), F32))

    def normed(c):
        o = opart_scr[c] + lax.dot_general(qhat_scr[c], st_scr[c].astype(BF16),
                                           (((1,), (1,)), ((), ())), preferred_element_type=F32)
        return _rmsnorm_rows(o, hg_ref[...])

    om_ref[...] = (normed(0)[pad:, :] * _silu(rm_ref[...].astype(F32))).astype(BF16)

    def finish_main(c, carry):
        sl = pl.ds(pl.multiple_of(c * CHUNK, CHUNK), CHUNK)
        o_ref[sl, :] = (normed(c + 1) * _silu(r_ref[sl, :].astype(F32))).astype(BF16)
        return carry
    lax.fori_loop(0, n_main, finish_main, 0)


def _level_matrix():
    half = CHUNK // 2
    t = np.arange(half)
    x = t[:, None] ^ t[None, :]
    lv = np.full((half, half), -1, np.int32)
    nz = x > 0
    lv[nz] = np.floor(np.log2(x[nz])).astype(np.int32)
    return jnp.asarray(lv)


def _gla(p_main, a_main, p_meta, a_meta, w2blk, b2blk, head_g, *, bsz, seq, shared_meta):
    n = seq // CHUNK + 1
    qo, ko = Q_OFF // GLA_DK, K_OFF // GLA_DK
    vo, ro = V_OFF // GLA_DV, R_OFF // GLA_DV
    mrow = (lambda b: 0) if shared_meta else (lambda b: b)
    main = lambda width, off: pl.BlockSpec((seq, width), lambda b, h: (b, off + h))
    meta = lambda width, off: pl.BlockSpec((N_META, width), lambda b, h: (mrow(b), off + h))
    return pl.pallas_call(
        _gla_body,
        grid=(bsz, GLA_HEADS),
        in_specs=[main(GLA_DK, qo), main(GLA_DK, ko), main(GLA_DV, vo), main(GLA_DV, ro),
                  pl.BlockSpec((seq, A_COLS), lambda b, h: (b, 0)),
                  meta(GLA_DK, qo), meta(GLA_DK, ko), meta(GLA_DV, vo), meta(GLA_DV, ro),
                  pl.BlockSpec((N_META, A_COLS), lambda b, h: (mrow(b), 0)),
                  pl.BlockSpec((None, A_COLS, 2 * GLA_DK), lambda b, h: (h, 0, 0)),
                  pl.BlockSpec((None, 1, 2 * GLA_DK), lambda b, h: (h, 0, 0)),
                  pl.BlockSpec((1, GLA_DV), lambda b, h: (0, 0)),
                  pl.BlockSpec((CHUNK // 2, CHUNK // 2), lambda b, h: (0, 0))],
        out_specs=[pl.BlockSpec((seq, GLA_DV), lambda b, h: (b, h)),
                   pl.BlockSpec((N_META, GLA_DV), lambda b, h: (b, h))],
        out_shape=[jax.ShapeDtypeStruct((bsz * seq, GLA_WIDTH), BF16),
                   jax.ShapeDtypeStruct((bsz * N_META, GLA_WIDTH), BF16)],
        scratch_shapes=[pltpu.VMEM((n, CHUNK, GLA_DV), F32),
                        pltpu.VMEM((n, CHUNK, 2 * GLA_DK), BF16),
                        pltpu.VMEM((n, GLA_DV, 2 * GLA_DK), F32),
                        pltpu.VMEM((n, 1, 2 * GLA_DK), F32)],
        compiler_params=_params("parallel", "parallel"),
        name="gla",
    )(p_main, p_main, p_main, p_main, a_main, p_meta, p_meta, p_meta, p_meta, a_meta,
      w2blk, b2blk, head_g, _level_matrix())


def _conv_rows(y_scr, w, start, rows):
    n = rows + 2 * HALO
    win = y_scr[pl.ds(start - HALO, n), :]
    prev = pltpu.roll(win, 1, 0)[HALO:HALO + rows]
    nxt = pltpu.roll(win, n - 1, 0)[HALO:HALO + rows]
    return prev * w[0:1] + win[HALO:HALO + rows] * w[1:2] + nxt * w[2:3]


def _frame(y_scr, y_meta, seq):
    zeros = jnp.zeros((HALO, y_scr.shape[1]), F32)
    y_scr[0:HALO, :] = zeros
    y_scr[HALO:2 * HALO, :] = y_meta
    y_scr[2 * HALO + seq:3 * HALO + seq, :] = zeros


def _mixconv_body(cb_ref, cc_ref, ch_ref, cbm_ref, ccm_ref, chm_ref, w_ref, o_ref, om_ref, y_scr,
                  *, rows):
    seq = cb_ref.shape[0]
    w = w_ref[...]
    _frame(y_scr, ccm_ref[...].astype(F32) * chm_ref[...].astype(F32), seq)

    def fill(i, carry):
        sl = pl.ds(pl.multiple_of(i * rows, rows), rows)
        dst = pl.ds(pl.multiple_of(2 * HALO + i * rows, HALO), rows)
        y_scr[dst, :] = cc_ref[sl, :].astype(F32) * ch_ref[sl, :].astype(F32)
        return carry
    lax.fori_loop(0, seq // rows, fill, 0)

    om_ref[...] = (cbm_ref[...].astype(F32) * _conv_rows(y_scr, w, HALO, N_META)).astype(BF16)

    def tile(i, carry):
        sl = pl.ds(pl.multiple_of(i * rows, rows), rows)
        start = pl.multiple_of(2 * HALO + i * rows, HALO)
        o_ref[sl, :] = (cb_ref[sl, :].astype(F32) * _conv_rows(y_scr, w, start, rows)).astype(BF16)
        return carry
    lax.fori_loop(0, seq // rows, tile, 0)


def _mixconv(p_main, p_meta, conv_w, *, bsz, seq, shared_meta, rows=256):
    cbo, cco, cho = CB_OFF // CONV_COLS, CC_OFF // CONV_COLS, CH_OFF // CONV_COLS
    mrow = (lambda b: 0) if shared_meta else (lambda b: b)
    main = lambda off: pl.BlockSpec((seq, CONV_COLS), lambda b, j: (b, off + j))
    meta = lambda off: pl.BlockSpec((N_META, CONV_COLS), lambda b, j: (mrow(b), off + j))
    return pl.pallas_call(
        functools.partial(_mixconv_body, rows=rows),
        grid=(bsz, CONV_WIDTH // CONV_COLS),
        in_specs=[main(cbo), main(cco), main(cho), meta(cbo), meta(cco), meta(cho),
                  pl.BlockSpec((3, CONV_COLS), lambda b, j: (0, j))],
        out_specs=[pl.BlockSpec((seq, CONV_COLS), lambda b, j: (b, j)),
                   pl.BlockSpec((N_META, CONV_COLS), lambda b, j: (b, j))],
        out_shape=[jax.ShapeDtypeStruct((bsz * seq, CONV_WIDTH), BF16),
                   jax.ShapeDtypeStruct((bsz * N_META, CONV_WIDTH), BF16)],
        scratch_shapes=[pltpu.VMEM((seq + 3 * HALO, CONV_COLS), F32)],
        compiler_params=_params("parallel", "parallel"),
        name="mixconv",
    )(p_main, p_main, p_main, p_meta, p_meta, p_meta, conv_w)


def _norm_block_rows(src_ref, dst_ref, g_ref, rows):
    def body(i, carry):
        sl = pl.ds(pl.multiple_of(i * rows, rows), rows)
        dst_ref[sl, :] = _rmsnorm_rows(src_ref[sl, :], g_ref[...]).astype(dst_ref.dtype)
        return carry
    lax.fori_loop(0, src_ref.shape[0] // rows, body, 0)


def _out_proj_body(o_ref, c_ref, wo_ref, wc_ref, x_ref, g_ref, x1_ref, h2_ref, *, tn, rows):
    j = pl.program_id(1)
    acc = jnp.dot(o_ref[...], wo_ref[...], preferred_element_type=F32)
    acc = acc + jnp.dot(c_ref[...], wc_ref[...], preferred_element_type=F32)
    x1_ref[:, pl.ds(pl.multiple_of(j * tn, tn), tn)] = x_ref[...] + acc

    @pl.when(j == pl.num_programs(1) - 1)
    def _():
        _norm_block_rows(x1_ref, h2_ref, g_ref, rows)


def _out_proj(o, conv, w_out, x, g, *, tm):
    m = x.shape[0]
    tn = w_out.shape[2]
    return pl.pallas_call(
        functools.partial(_out_proj_body, tn=tn, rows=min(32, tm)),
        grid=(m // tm, D_MODEL // tn),
        in_specs=[pl.BlockSpec((tm, GLA_WIDTH), lambda i, j: (i, 0)),
                  pl.BlockSpec((tm, CONV_WIDTH), lambda i, j: (i, 0)),
                  pl.BlockSpec((None, GLA_WIDTH, tn), lambda i, j: (j, 0, 0)),
                  pl.BlockSpec((None, CONV_WIDTH, tn), lambda i, j: (j, 1, 0)),
                  pl.BlockSpec((tm, tn), lambda i, j: (i, j)),
                  pl.BlockSpec((1, D_MODEL), lambda i, j: (0, 0))],
        out_specs=[pl.BlockSpec((tm, D_MODEL), lambda i, j: (i, 0)),
                   pl.BlockSpec((tm, D_MODEL), lambda i, j: (i, 0))],
        out_shape=[jax.ShapeDtypeStruct((m, D_MODEL), F32),
                   jax.ShapeDtypeStruct((m, D_MODEL), BF16)],
        compiler_params=_params("parallel", "arbitrary"),
        name="out_proj",
    )(o, conv, w_out, w_out, x, g)


def _up_proj_body(h_ref, hp_ref, hn_ref, hm_ref, wu_ref, wg_ref, cw_ref, cb_ref, act_ref, lhs_scr,
                  *, tiles_per_seq, rows):
    tm = h_ref.shape[0]
    i = pl.program_id(0)

    @pl.when(pl.program_id(1) == 0)
    def _():
        first = (i % tiles_per_seq) == 0
        last = (i % tiles_per_seq) == tiles_per_seq - 1

        @pl.when(first)
        def _():
            lhs_scr[0:HALO, :] = hm_ref[...]

        @pl.when(jnp.logical_not(first))
        def _():
            lhs_scr[0:HALO, :] = hp_ref[...]

        @pl.when(last)
        def _():
            lhs_scr[HALO + tm:2 * HALO + tm, :] = jnp.zeros((HALO, D_MODEL), BF16)

        @pl.when(jnp.logical_not(last))
        def _():
            lhs_scr[HALO + tm:2 * HALO + tm, :] = hn_ref[...]

        def copy_rows(r, carry):
            sl = pl.ds(pl.multiple_of(r * rows, rows), rows)
            lhs_scr[pl.ds(pl.multiple_of(HALO + r * rows, HALO), rows), :] = h_ref[sl, :]
            return carry
        lax.fori_loop(0, tm // rows, copy_rows, 0)

    n = tm + 2 * HALO
    g = jnp.dot(lhs_scr[...], wg_ref[...], preferred_element_type=F32)
    u = jnp.dot(h_ref[...], wu_ref[...], preferred_element_type=F32)
    w = cw_ref[...]
    conv = (pltpu.roll(g, 1, 0)[HALO:HALO + tm] * w[0:1] + g[HALO:HALO + tm] * w[1:2]
            + pltpu.roll(g, n - 1, 0)[HALO:HALO + tm] * w[2:3])
    act_ref[...] = (_silu(conv + cb_ref[...]) * u).astype(BF16)


def _up_proj(h_main, h_meta, w_up, conv_w, conv_b, *, seq, meta_off, tm):
    m = h_main.shape[0]
    tn = w_up.shape[2]
    nblk = D_FF // tn
    tiles_per_seq = seq // tm
    halo_per_tile = tm // HALO
    n_halo = m // HALO
    return pl.pallas_call(
        functools.partial(_up_proj_body, tiles_per_seq=tiles_per_seq, rows=64),
        grid=(m // tm, nblk),
        in_specs=[pl.BlockSpec((tm, D_MODEL), lambda i, j: (i, 0)),
                  pl.BlockSpec((HALO, D_MODEL),
                               lambda i, j: (jnp.maximum(i * halo_per_tile - 1, 0), 0)),
                  pl.BlockSpec((HALO, D_MODEL),
                               lambda i, j: (jnp.minimum((i + 1) * halo_per_tile, n_halo - 1), 0)),
                  pl.BlockSpec((HALO, D_MODEL), lambda i, j: (meta_off + i // tiles_per_seq, 0)),
                  pl.BlockSpec((None, D_MODEL, tn), lambda i, j: (j, 0, 0)),
                  pl.BlockSpec((None, D_MODEL, tn), lambda i, j: (nblk + j, 0, 0)),
                  pl.BlockSpec((3, tn), lambda i, j: (0, j)),
                  pl.BlockSpec((1, tn), lambda i, j: (0, j))],
        out_specs=pl.BlockSpec((tm, tn), lambda i, j: (i, j)),
        out_shape=jax.ShapeDtypeStruct((m, D_FF), BF16),
        scratch_shapes=[pltpu.VMEM((tm + 2 * HALO, D_MODEL), BF16)],
        compiler_params=_params("parallel", "arbitrary"),
        name="up_proj",
    )(h_main, h_main, h_main, h_meta, w_up, w_up, conv_w, conv_b)


def _down_proj_body(act_ref, w_ref, x1_ref, g_ref, y_ref, *, tn, rows):
    j = pl.program_id(1)
    y_ref[:, pl.ds(pl.multiple_of(j * tn, tn), tn)] = (
        x1_ref[...] + jnp.dot(act_ref[...], w_ref[...], preferred_element_type=F32))

    @pl.when(j == pl.num_programs(1) - 1)
    def _():
        _norm_block_rows(y_ref, y_ref, g_ref, rows)


def _down_proj(act, w_down, x1, g, *, tm):
    m = act.shape[0]
    tn = w_down.shape[2]
    return pl.pallas_call(
        functools.partial(_down_proj_body, tn=tn, rows=32),
        grid=(m // tm, D_MODEL // tn),
        in_specs=[pl.BlockSpec((tm, D_FF), lambda i, j: (i, 0)),
                  pl.BlockSpec((None, D_FF, tn), lambda i, j: (j, 0, 0)),
                  pl.BlockSpec((tm, tn), lambda i, j: (i, j)),
                  pl.BlockSpec((1, D_MODEL), lambda i, j: (0, 0))],
        out_specs=pl.BlockSpec((tm, D_MODEL), lambda i, j: (i, 0)),
        out_shape=jax.ShapeDtypeStruct((m, D_MODEL), F32),
        compiler_params=_params("parallel", "arbitrary"),
        name="down_proj",
    )(act, w_down, x1, g)


def _prepare_weights(mix_norm_g, w_in, w_gate2, b_gate2, head_norm_g, conv_mix_w, w_out,
                     ffn_norm_g, w_up, ffn_conv_w, ffn_conv_b, w_down, final_norm_g):
    def col_blocks(wt, tn):
        k, n = wt.shape
        return wt.reshape(k, n // tn, tn).transpose(1, 0, 2).astype(BF16)

    a_lo = V_OFF + 2 * GLA_WIDTH
    w_main = jnp.concatenate([col_blocks(w_in[:, :a_lo], IN_PROJ_TN),
                              col_blocks(w_in[:, a_lo + A_COLS:], IN_PROJ_TN)], axis=0)
    w_a = w_in[:, a_lo:a_lo + A_COLS].astype(BF16)
    wf = w_gate2[0].reshape(GATE_RANK, GLA_HEADS, GLA_DK).transpose(1, 0, 2)
    wb = w_gate2[1].reshape(GATE_RANK, GLA_HEADS, GLA_DK).transpose(1, 0, 2)
    zero = jnp.zeros_like(wf)
    w2blk = jnp.concatenate([jnp.concatenate([wf, zero], axis=2),
                             jnp.concatenate([zero, wb], axis=2)], axis=1).astype(BF16)
    b2blk = jnp.concatenate([b_gate2[0].reshape(GLA_HEADS, 1, GLA_DK),
                             b_gate2[1].reshape(GLA_HEADS, 1, GLA_DK)], axis=2).astype(F32)
    return dict(
        mix_g=mix_norm_g.reshape(1, D_MODEL), w_main=w_main, w_a=w_a, w2blk=w2blk, b2blk=b2blk,
        head_g=head_norm_g.reshape(1, GLA_DV), conv_mix_w=conv_mix_w,
        w_out=col_blocks(w_out, OUT_PROJ_TN),
        ffn_g=ffn_norm_g.reshape(1, D_MODEL), w_up=col_blocks(w_up, UP_PROJ_TN),
        ffn_conv_w=ffn_conv_w,
        ffn_conv_b=ffn_conv_b.reshape(1, D_FF), w_down=col_blocks(w_down, DOWN_PROJ_TN),
        final_g=final_norm_g.reshape(1, D_MODEL))


def kernel(x_prompt, x_sample, meta_tokens, mix_norm_g, w_in, w_gate2, b_gate2, head_norm_g,
           conv_mix_w, w_out, ffn_norm_g, w_up, ffn_conv_w, ffn_conv_b, w_down, final_norm_g):
    w = _prepare_weights(mix_norm_g[0], w_in[0], w_gate2[0], b_gate2[0], head_norm_g[0],
                         conv_mix_w[0], w_out[0], ffn_norm_g[0], w_up[0], ffn_conv_w[0],
                         ffn_conv_b[0], w_down[0], final_norm_g)
    trunks = (x_prompt, x_sample)
    p_meta16, a_meta16 = _in_proj(meta_tokens, w["mix_g"], w["w_main"], w["w_a"], tm=N_META)

    mixed = []
    for x in trunks:
        bsz, seq, _ = x.shape
        x_main = x.reshape(bsz * seq, D_MODEL)
        p_main, a_main = _in_proj(x_main, w["mix_g"], w["w_main"], w["w_a"], tm=512)
        o_main, o_meta = _gla(p_main, a_main, p_meta16, a_meta16, w["w2blk"], w["b2blk"],
                              w["head_g"], bsz=bsz, seq=seq, shared_meta=True)
        c_main, c_meta = _mixconv(p_main, p_meta16, w["conv_mix_w"], bsz=bsz, seq=seq,
                                  shared_meta=True)
        mixed.append((x_main, o_main, c_main, o_meta, c_meta))

    n_seq = sum(x.shape[0] for x in trunks)
    _, h2_meta = _out_proj(jnp.concatenate([t[3] for t in mixed], axis=0),
                           jnp.concatenate([t[4] for t in mixed], axis=0),
                           w["w_out"], jnp.tile(meta_tokens, (n_seq, 1)), w["ffn_g"],
                           tm=n_seq * N_META)

    outs = []
    meta_off = 0
    for x, (x_main, o_main, c_main, _, _) in zip(trunks, mixed):
        bsz, seq, _ = x.shape
        x1, h2 = _out_proj(o_main, c_main, w["w_out"], x_main, w["ffn_g"], tm=512)
        act = _up_proj(h2, h2_meta, w["w_up"], w["ffn_conv_w"], w["ffn_conv_b"],
                       seq=seq, meta_off=meta_off, tm=1024)
        y = _down_proj(act, w["w_down"], x1, w["final_g"], tm=512)
        outs.append(y.reshape(bsz, seq, D_MODEL))
        meta_off += bsz
    return tuple(outs)
```

```python
import functools

import numpy as np
import jax
import jax.numpy as jnp
from jax import lax
from jax.experimental import pallas as pl
from jax.experimental.pallas import tpu as pltpu

F32 = jnp.float32
BF16 = jnp.bfloat16

D_MODEL = 4096
N_META = 16
GLA_HEADS = 8
GLA_DK = 128
GLA_DV = 256
GLA_KEY_WIDTH = GLA_HEADS * GLA_DK
GLA_WIDTH = GLA_HEADS * GLA_DV
CONV_WIDTH = 2048
GATE_RANK = 16
GATE_TAU = 16.0
D_FF = 11008
EPS = 1e-6

P_COLS = 2 * GLA_KEY_WIDTH + 2 * GLA_WIDTH + 3 * CONV_WIDTH
A_COLS = 2 * GATE_RANK
Q_OFF, K_OFF, V_OFF, R_OFF = 0, GLA_KEY_WIDTH, 2 * GLA_KEY_WIDTH, 2 * GLA_KEY_WIDTH + GLA_WIDTH
CB_OFF = R_OFF + GLA_WIDTH
CC_OFF = CB_OFF + CONV_WIDTH
CH_OFF = CC_OFF + CONV_WIDTH

CHUNK = 256
LEVELS = 8
F32_TILE_ROWS = 8
BF16_TILE_ROWS = 16
TILE_LEVELS = 3
LOG2_E = 1.4426950408889634
HALO = 16
CONV_COLS = 256
NORM_ROWS = 128
V7X_VMEM_LIMIT = 56 * 1024 * 1024
IN_PROJ_TN = 1024
OUT_PROJ_TN = 512
UP_PROJ_TN = 256
DOWN_PROJ_TN = 256


def _params(*sem):
    return pltpu.CompilerParams(dimension_semantics=sem, vmem_limit_bytes=V7X_VMEM_LIMIT)


def _rmsnorm_rows(x, g):
    ms = jnp.mean(x * x, axis=-1, keepdims=True)
    return x * lax.rsqrt(ms + EPS) * g


def _silu(x):
    return x * (1.0 / (1.0 + jnp.exp(-x)))


def _in_proj_body(x_ref, g_ref, w_ref, wa_ref, p_ref, a_ref, h_scr, *, rows):
    @pl.when(pl.program_id(1) == 0)
    def _():
        def norm_rows(i, carry):
            sl = pl.ds(pl.multiple_of(i * rows, rows), rows)
            h_scr[sl, :] = _rmsnorm_rows(x_ref[sl, :], g_ref[...]).astype(BF16)
            return carry
        lax.fori_loop(0, x_ref.shape[0] // rows, norm_rows, 0)
        a_ref[...] = jnp.dot(h_scr[...], wa_ref[...], preferred_element_type=F32)

    p_ref[...] = jnp.dot(h_scr[...], w_ref[...], preferred_element_type=F32).astype(BF16)


def _in_proj(x, g, w_main, w_a, *, tm):
    m = x.shape[0]
    tn = IN_PROJ_TN
    return pl.pallas_call(
        functools.partial(_in_proj_body, rows=min(NORM_ROWS, tm)),
        grid=(m // tm, P_COLS // tn),
        in_specs=[pl.BlockSpec((tm, D_MODEL), lambda i, j: (i, 0), pipeline_mode=pl.Buffered(1)),
                  pl.BlockSpec((1, D_MODEL), lambda i, j: (0, 0)),
                  pl.BlockSpec((D_MODEL, tn), lambda i, j: (0, j)),
                  pl.BlockSpec((D_MODEL, A_COLS), lambda i, j: (0, 0))],
        out_specs=[pl.BlockSpec((tm, tn), lambda i, j: (i, j)),
                   pl.BlockSpec((tm, A_COLS), lambda i, j: (i, 0))],
        out_shape=[jax.ShapeDtypeStruct((m, P_COLS), BF16),
                   jax.ShapeDtypeStruct((m, A_COLS), F32)],
        scratch_shapes=[pltpu.VMEM((tm, D_MODEL), BF16)],
        compiler_params=_params("parallel", "arbitrary"),
        name="in_proj",
    )(x, g, w_main, w_a)


def _log2_decay(a, w2, b2):
    z = jnp.dot(a.astype(BF16), w2, preferred_element_type=F32) + b2
    return (jnp.minimum(z, 0.0) - jnp.log(1.0 + jnp.exp(-jnp.abs(z)))) * (LOG2_E / GATE_TAU)


def _nt_dot(a, b):
    return lax.dot_general(a, b, (((1,), (1,)), ((), ())), preferred_element_type=F32)


def _chunk_intra(q, k, v, lg, lvd):
    half = CHUNK // 2
    shape = (CHUNK, 2 * GLA_DK)
    row = lax.broadcasted_iota(jnp.int32, shape, 0)
    isb = lax.broadcasted_iota(jnp.int32, shape, 1) >= GLA_DK
    qb = q.astype(BF16)
    q2b = jnp.concatenate([qb, qb], axis=1)
    kb = k.astype(BF16)
    k2b = jnp.concatenate([kb, kb], axis=1)
    zero = jnp.zeros(shape, BF16)
    diag = [jnp.zeros((half, half), F32), jnp.zeros((half, half), F32)]

    def keep_level(lvl, ql, kl):
        for d in range(2):
            rows = slice(d * half, (d + 1) * half)
            diag[d] = jnp.where(lvd == lvl, _nt_dot(ql[rows], kl[rows]), diag[d])

    def masked_level(lvl, e):
        upper = (row & (1 << lvl)) != 0
        qside = jnp.logical_xor(upper, isb)
        keep_level(lvl, jnp.where(qside, q2b * e, zero), k2b * e)
        return upper

    n_tiles = CHUNK // F32_TILE_ROWS
    tiled = (n_tiles, F32_TILE_ROWS, 2 * GLA_DK)
    row_t = lax.broadcasted_iota(jnp.int32, (1,) + tiled[1:], 1)
    isb_t = lax.broadcasted_iota(jnp.int32, (1,) + tiled[1:], 2) >= GLA_DK
    tb = lg.reshape(tiled)
    p = jnp.where(isb_t, 0.0, tb)
    for lvl in range(TILE_LEVELS):
        m = 1 << lvl
        upper_t = (row_t & m) != 0
        e = jnp.exp2(jnp.where(upper_t, p, tb - p)).reshape(shape).astype(BF16)
        masked_level(lvl, e)
        tb_prev = pltpu.roll(tb, m, 1)
        tb_next = pltpu.roll(tb, F32_TILE_ROWS - m, 1)
        p = p + jnp.where(upper_t, tb_prev, 0.0)
        tb = tb + jnp.where(upper_t, tb_prev, tb_next)

    off = [jnp.zeros(tiled[1:], F32)]
    for j in range(n_tiles - 1):
        off.append(off[-1] + tb[j])
    total = off[-1] + tb[n_tiles - 1]
    pre_t = [p[j] + off[j] for j in range(n_tiles)]

    for lvl in range(TILE_LEVELS, LEVELS):
        m = 1 << lvl
        tpb = m // F32_TILE_ROWS
        args = []
        for j in range(n_tiles):
            later_start = off[(j // (2 * tpb) * 2 + 1) * tpb]
            args.append(pre_t[j] - later_start if (j // tpb) % 2 else later_start - pre_t[j])
        e = jnp.exp2(jnp.concatenate(args, axis=0)).astype(BF16)
        if m < BF16_TILE_ROWS:
            masked_level(lvl, e)
        elif m < half:
            zeros_m = jnp.zeros((m, GLA_DK), BF16)
            qf, qr = [], []
            for g in range(CHUNK // (2 * m)):
                lo = slice(2 * m * g, 2 * m * g + m)
                up = slice(2 * m * g + m, 2 * m * (g + 1))
                qf += [zeros_m, qb[up] * e[up, :GLA_DK]]
                qr += [qb[lo] * e[lo, GLA_DK:], zeros_m]
            ql = jnp.concatenate([jnp.concatenate(qf, axis=0), jnp.concatenate(qr, axis=0)], axis=1)
            keep_level(lvl, ql, k2b * e)
        else:
            s_lower = _nt_dot(qb[half:] * e[half:, :GLA_DK], kb[:half] * e[:half, :GLA_DK])
            s_upper = _nt_dot(qb[:half] * e[:half, GLA_DK:], kb[half:] * e[half:, GLA_DK:])
    scores = jnp.concatenate([jnp.concatenate([diag[0], s_upper], axis=1),
                              jnp.concatenate([s_lower, diag[1]], axis=1)], axis=0)

    vb = v.astype(BF16)
    qk = jnp.sum(q * k, axis=-1, keepdims=True)
    o_part = jnp.dot(scores.astype(BF16), vb, preferred_element_type=F32) + qk * v
    pre = jnp.concatenate(pre_t, axis=0)
    rest = jnp.concatenate([total] * n_tiles, axis=0) - pre
    khat = k2b * jnp.exp2(jnp.concatenate([rest[:, :GLA_DK], pre[:, GLA_DK:]], axis=1)).astype(BF16)
    qhat = q2b * jnp.exp2(jnp.concatenate([pre[:, :GLA_DK], rest[:, GLA_DK:]], axis=1)).astype(BF16)
    dstate = lax.dot_general(vb, khat, (((0,), (0,)), ((), ())), preferred_element_type=F32)
    return o_part, qhat, dstate, jnp.exp2(total[0:1, :])


def _gla_body(q_ref, k_ref, v_ref, r_ref, a_ref, qm_ref, km_ref, vm_ref, rm_ref, am_ref,
              w2_ref, b2_ref, hg_ref, lv_ref, o_ref, om_ref,
              opart_scr, qhat_scr, st_scr, dec_scr):
    n_main = q_ref.shape[0] // CHUNK
    n = n_main + 1
    scale = GLA_DK ** -0.5
    w2 = w2_ref[...]
    b2 = b2_ref[...]
    pad = CHUNK - N_META

    def intra(c, q, k, v, lg):
        o_part, qhat, dstate, dec = _chunk_intra(q * scale, k, v, lg, lv_ref[...])
        opart_scr[c] = o_part
        qhat_scr[c] = qhat
        st_scr[c] = dstate
        dec_scr[c] = dec

    def behind_zeros(x):
        return jnp.concatenate([jnp.zeros((pad, x.shape[1]), F32), x.astype(F32)], axis=0)

    intra(0, behind_zeros(qm_ref[...]), behind_zeros(km_ref[...]), behind_zeros(vm_ref[...]),
          behind_zeros(_log2_decay(am_ref[...], w2, b2)))

    def intra_main(c, carry):
        sl = pl.ds(pl.multiple_of(c * CHUNK, CHUNK), CHUNK)
        intra(c + 1, q_ref[sl, :].astype(F32), k_ref[sl, :].astype(F32), v_ref[sl, :].astype(F32),
              _log2_decay(a_ref[sl, :], w2, b2))
        return carry
    lax.fori_loop(0, n_main, intra_main, 0)

    def scan_fwd(c, s):
        d = st_scr[c, :, 0:GLA_DK]
        st_scr[c, :, 0:GLA_DK] = s
        return dec_scr[c][:, 0:GLA_DK] * s + d
    lax.fori_loop(0, n, scan_fwd, jnp.zeros((GLA_DV, GLA_DK), F32))

    def scan_bwd(i, s):
        c = n - 1 - i
        d = st_scr[c, :, GLA_DK:]
        st_scr[c, :, GLA_DK:] = s
        return dec_scr[c][:, GLA_DK:] * s + d
    lax.fori_loop(0, n, scan_bwd, jnp.zeros((GLA_DV, GLA_DK), F32))

    def normed(c):
        o = opart_scr[c] + lax.dot_general(qhat_scr[c], st_scr[c].astype(BF16),
                                           (((1,), (1,)), ((), ())), preferred_element_type=F32)
        return _rmsnorm_rows(o, hg_ref[...])

    om_ref[...] = (normed(0)[pad:, :] * _silu(rm_ref[...].astype(F32))).astype(BF16)

    def finish_main(c, carry):
        sl = pl.ds(pl.multiple_of(c * CHUNK, CHUNK), CHUNK)
        o_ref[sl, :] = (normed(c + 1) * _silu(r_ref[sl, :].astype(F32))).astype(BF16)
        return carry
    lax.fori_loop(0, n_main, finish_main, 0)


def _level_matrix():
    half = CHUNK // 2
    t = np.arange(half)
    x = t[:, None] ^ t[None, :]
    lv = np.full((half, half), -1, np.int32)
    nz = x > 0
    lv[nz] = np.floor(np.log2(x[nz])).astype(np.int32)
    return jnp.asarray(lv)


def _gla(p_main, a_main, p_meta, a_meta, w2blk, b2blk, head_g, *, bsz, seq, shared_meta):
    n = seq // CHUNK + 1
    qo, ko = Q_OFF // GLA_DK, K_OFF // GLA_DK
    vo, ro = V_OFF // GLA_DV, R_OFF // GLA_DV
    mrow = (lambda b: 0) if shared_meta else (lambda b: b)
    main = lambda width, off: pl.BlockSpec((seq, width), lambda b, h: (b, off + h))
    meta = lambda width, off: pl.BlockSpec((N_META, width), lambda b, h: (mrow(b), off + h))
    return pl.pallas_call(
        _gla_body,
        grid=(bsz, GLA_HEADS),
        in_specs=[main(GLA_DK, qo), main(GLA_DK, ko), main(GLA_DV, vo), main(GLA_DV, ro),
                  pl.BlockSpec((seq, A_COLS), lambda b, h: (b, 0)),
                  meta(GLA_DK, qo), meta(GLA_DK, ko), meta(GLA_DV, vo), meta(GLA_DV, ro),
                  pl.BlockSpec((N_META, A_COLS), lambda b, h: (mrow(b), 0)),
                  pl.BlockSpec((None, A_COLS, 2 * GLA_DK), lambda b, h: (h, 0, 0)),
                  pl.BlockSpec((None, 1, 2 * GLA_DK), lambda b, h: (h, 0, 0)),
                  pl.BlockSpec((1, GLA_DV), lambda b, h: (0, 0)),
                  pl.BlockSpec((CHUNK // 2, CHUNK // 2), lambda b, h: (0, 0))],
        out_specs=[pl.BlockSpec((seq, GLA_DV), lambda b, h: (b, h)),
                   pl.BlockSpec((N_META, GLA_DV), lambda b, h: (b, h))],
        out_shape=[jax.ShapeDtypeStruct((bsz * seq, GLA_WIDTH), BF16),
                   jax.ShapeDtypeStruct((bsz * N_META, GLA_WIDTH), BF16)],
        scratch_shapes=[pltpu.VMEM((n, CHUNK, GLA_DV), F32),
                        pltpu.VMEM((n, CHUNK, 2 * GLA_DK), BF16),
                        pltpu.VMEM((n, GLA_DV, 2 * GLA_DK), F32),
                        pltpu.VMEM((n, 1, 2 * GLA_DK), F32)],
        compiler_params=_params("parallel", "parallel"),
        name="gla",
    )(p_main, p_main, p_main, p_main, a_main, p_meta, p_meta, p_meta, p_meta, a_meta,
      w2blk, b2blk, head_g, _level_matrix())


def _conv_rows(y_scr, w, start, rows):
    n = rows + 2 * HALO
    win = y_scr[pl.ds(start - HALO, n), :]
    prev = pltpu.roll(win, 1, 0)[HALO:HALO + rows]
    nxt = pltpu.roll(win, n - 1, 0)[HALO:HALO + rows]
    return prev * w[0:1] + win[HALO:HALO + rows] * w[1:2] + nxt * w[2:3]


def _frame(y_scr, y_meta, seq):
    zeros = jnp.zeros((HALO, y_scr.shape[1]), F32)
    y_scr[0:HALO, :] = zeros
    y_scr[HALO:2 * HALO, :] = y_meta
    y_scr[2 * HALO + seq:3 * HALO + seq, :] = zeros


def _mixconv_body(cb_ref, cc_ref, ch_ref, cbm_ref, ccm_ref, chm_ref, w_ref, o_ref, om_ref, y_scr,
                  *, rows):
    seq = cb_ref.shape[0]
    w = w_ref[...]
    _frame(y_scr, ccm_ref[...].astype(F32) * chm_ref[...].astype(F32), seq)

    def fill(i, carry):
        sl = pl.ds(pl.multiple_of(i * rows, rows), rows)
        dst = pl.ds(pl.multiple_of(2 * HALO + i * rows, HALO), rows)
        y_scr[dst, :] = cc_ref[sl, :].astype(F32) * ch_ref[sl, :].astype(F32)
        return carry
    lax.fori_loop(0, seq // rows, fill, 0)

    om_ref[...] = (cbm_ref[...].astype(F32) * _conv_rows(y_scr, w, HALO, N_META)).astype(BF16)

    def tile(i, carry):
        sl = pl.ds(pl.multiple_of(i * rows, rows), rows)
        start = pl.multiple_of(2 * HALO + i * rows, HALO)
        o_ref[sl, :] = (cb_ref[sl, :].astype(F32) * _conv_rows(y_scr, w, start, rows)).astype(BF16)
        return carry
    lax.fori_loop(0, seq // rows, tile, 0)


def _mixconv(p_main, p_meta, conv_w, *, bsz, seq, shared_meta, rows=256):
    cbo, cco, cho = CB_OFF // CONV_COLS, CC_OFF // CONV_COLS, CH_OFF // CONV_COLS
    mrow = (lambda b: 0) if shared_meta else (lambda b: b)
    main = lambda off: pl.BlockSpec((seq, CONV_COLS), lambda b, j: (b, off + j))
    meta = lambda off: pl.BlockSpec((N_META, CONV_COLS), lambda b, j: (mrow(b), off + j))
    return pl.pallas_call(
        functools.partial(_mixconv_body, rows=rows),
        grid=(bsz, CONV_WIDTH // CONV_COLS),
        in_specs=[main(cbo), main(cco), main(cho), meta(cbo), meta(cco), meta(cho),
                  pl.BlockSpec((3, CONV_COLS), lambda b, j: (0, j))],
        out_specs=[pl.BlockSpec((seq, CONV_COLS), lambda b, j: (b, j)),
                   pl.BlockSpec((N_META, CONV_COLS), lambda b, j: (b, j))],
        out_shape=[jax.ShapeDtypeStruct((bsz * seq, CONV_WIDTH), BF16),
                   jax.ShapeDtypeStruct((bsz * N_META, CONV_WIDTH), BF16)],
        scratch_shapes=[pltpu.VMEM((seq + 3 * HALO, CONV_COLS), F32)],
        compiler_params=_params("parallel", "parallel"),
        name="mixconv",
    )(p_main, p_main, p_main, p_meta, p_meta, p_meta, conv_w)


def _norm_block_rows(src_ref, dst_ref, g_ref, rows):
    def body(i, carry):
        sl = pl.ds(pl.multiple_of(i * rows, rows), rows)
        dst_ref[sl, :] = _rmsnorm_rows(src_ref[sl, :], g_ref[...]).astype(dst_ref.dtype)
        return carry
    lax.fori_loop(0, src_ref.shape[0] // rows, body, 0)


def _out_proj_body(o_ref, c_ref, wo_ref, wc_ref, x_ref, g_ref, x1_ref, h2_ref, *, tn, rows):
    j = pl.program_id(1)
    acc = jnp.dot(o_ref[...], wo_ref[...], preferred_element_type=F32)
    acc = acc + jnp.dot(c_ref[...], wc_ref[...], preferred_element_type=F32)
    x1_ref[:, pl.ds(pl.multiple_of(j * tn, tn), tn)] = x_ref[...] + acc

    @pl.when(j == pl.num_programs(1) - 1)
    def _():
        _norm_block_rows(x1_ref, h2_ref, g_ref, rows)


def _out_proj(o, conv, w_out, x, g, *, tm):
    m = x.shape[0]
    tn = OUT_PROJ_TN
    return pl.pallas_call(
        functools.partial(_out_proj_body, tn=tn, rows=min(NORM_ROWS, tm)),
        grid=(m // tm, D_MODEL // tn),
        in_specs=[pl.BlockSpec((tm, GLA_WIDTH), lambda i, j: (i, 0)),
                  pl.BlockSpec((tm, CONV_WIDTH), lambda i, j: (i, 0)),
                  pl.BlockSpec((GLA_WIDTH, tn), lambda i, j: (0, j)),
                  pl.BlockSpec((CONV_WIDTH, tn), lambda i, j: (1, j)),
                  pl.BlockSpec((tm, tn), lambda i, j: (i, j)),
                  pl.BlockSpec((1, D_MODEL), lambda i, j: (0, 0))],
        out_specs=[pl.BlockSpec((tm, D_MODEL), lambda i, j: (i, 0)),
                   pl.BlockSpec((tm, D_MODEL), lambda i, j: (i, 0))],
        out_shape=[jax.ShapeDtypeStruct((m, D_MODEL), F32),
                   jax.ShapeDtypeStruct((m, D_MODEL), BF16)],
        compiler_params=_params("parallel", "arbitrary"),
        name="out_proj",
    )(o, conv, w_out, w_out, x, g)


def _up_proj_body(h_ref, hp_ref, hn_ref, hm_ref, wu_ref, wg_ref, cw_ref, cb_ref, act_ref, lhs_scr,
                  *, tiles_per_seq, rows):
    tm = h_ref.shape[0]
    i = pl.program_id(0)

    @pl.when(pl.program_id(1) == 0)
    def _():
        first = (i % tiles_per_seq) == 0
        last = (i % tiles_per_seq) == tiles_per_seq - 1

        @pl.when(first)
        def _():
            lhs_scr[0:HALO, :] = hm_ref[...]

        @pl.when(jnp.logical_not(first))
        def _():
            lhs_scr[0:HALO, :] = hp_ref[...]

        @pl.when(last)
        def _():
            lhs_scr[HALO + tm:2 * HALO + tm, :] = jnp.zeros((HALO, D_MODEL), BF16)

        @pl.when(jnp.logical_not(last))
        def _():
            lhs_scr[HALO + tm:2 * HALO + tm, :] = hn_ref[...]

        def copy_rows(r, carry):
            sl = pl.ds(pl.multiple_of(r * rows, rows), rows)
            lhs_scr[pl.ds(pl.multiple_of(HALO + r * rows, HALO), rows), :] = h_ref[sl, :]
            return carry
        lax.fori_loop(0, tm // rows, copy_rows, 0)

    n = tm + 2 * HALO
    g = jnp.dot(lhs_scr[...], wg_ref[...], preferred_element_type=F32)
    u = jnp.dot(h_ref[...], wu_ref[...], preferred_element_type=F32)
    w = cw_ref[...]
    conv = (pltpu.roll(g, 1, 0)[HALO:HALO + tm] * w[0:1] + g[HALO:HALO + tm] * w[1:2]
            + pltpu.roll(g, n - 1, 0)[HALO:HALO + tm] * w[2:3])
    act_ref[...] = (_silu(conv + cb_ref[...]) * u).astype(BF16)


def _up_proj(h_main, h_meta, w_up, conv_w, conv_b, *, seq, meta_off, tm):
    m = h_main.shape[0]
    tn = UP_PROJ_TN
    nblk = D_FF // tn
    tiles_per_seq = seq // tm
    halo_per_tile = tm // HALO
    n_halo = m // HALO
    return pl.pallas_call(
        functools.partial(_up_proj_body, tiles_per_seq=tiles_per_seq, rows=64),
        grid=(m // tm, nblk),
        in_specs=[pl.BlockSpec((tm, D_MODEL), lambda i, j: (i, 0)),
                  pl.BlockSpec((HALO, D_MODEL),
                               lambda i, j: (jnp.maximum(i * halo_per_tile - 1, 0), 0)),
                  pl.BlockSpec((HALO, D_MODEL),
                               lambda i, j: (jnp.minimum((i + 1) * halo_per_tile, n_halo - 1), 0)),
                  pl.BlockSpec((HALO, D_MODEL), lambda i, j: (meta_off + i // tiles_per_seq, 0)),
                  pl.BlockSpec((D_MODEL, tn), lambda i, j: (0, j)),
                  pl.BlockSpec((D_MODEL, tn), lambda i, j: (0, nblk + j)),
                  pl.BlockSpec((3, tn), lambda i, j: (0, j)),
                  pl.BlockSpec((1, tn), lambda i, j: (0, j))],
        out_specs=pl.BlockSpec((tm, tn), lambda i, j: (i, j)),
        out_shape=jax.ShapeDtypeStruct((m, D_FF), BF16),
        scratch_shapes=[pltpu.VMEM((tm + 2 * HALO, D_MODEL), BF16)],
        compiler_params=_params("parallel", "arbitrary"),
        name="up_proj",
    )(h_main, h_main, h_main, h_meta, w_up, w_up, conv_w, conv_b)


def _down_proj_body(act_ref, w_ref, x1_ref, g_ref, y_ref, *, tn, rows):
    j = pl.program_id(1)
    y_ref[:, pl.ds(pl.multiple_of(j * tn, tn), tn)] = (
        x1_ref[...] + jnp.dot(act_ref[...], w_ref[...], preferred_element_type=F32))

    @pl.when(j == pl.num_programs(1) - 1)
    def _():
        _norm_block_rows(y_ref, y_ref, g_ref, rows)


def _down_proj(act, w_down, x1, g, *, tm):
    m = act.shape[0]
    tn = DOWN_PROJ_TN
    return pl.pallas_call(
        functools.partial(_down_proj_body, tn=tn, rows=NORM_ROWS),
        grid=(m // tm, D_MODEL // tn),
        in_specs=[pl.BlockSpec((tm, D_FF), lambda i, j: (i, 0)),
                  pl.BlockSpec((D_FF, tn), lambda i, j: (0, j)),
                  pl.BlockSpec((tm, tn), lambda i, j: (i, j)),
                  pl.BlockSpec((1, D_MODEL), lambda i, j: (0, 0))],
        out_specs=pl.BlockSpec((tm, D_MODEL), lambda i, j: (i, 0)),
        out_shape=jax.ShapeDtypeStruct((m, D_MODEL), F32),
        compiler_params=_params("parallel", "arbitrary"),
        name="down_proj",
    )(act, w_down, x1, g)


def _cast_in_weights_body(w_ref, wm_ref, wa_ref):
    a_lo = V_OFF + 2 * GLA_WIDTH
    wm_ref[:, :a_lo] = w_ref[:, :a_lo].astype(BF16)
    wa_ref[...] = w_ref[:, a_lo:a_lo + A_COLS].astype(BF16)
    wm_ref[:, a_lo:] = w_ref[:, a_lo + A_COLS:].astype(BF16)


def _cast_in_weights(w_in, *, tk=128):
    return pl.pallas_call(
        _cast_in_weights_body,
        grid=(D_MODEL // tk,),
        in_specs=[pl.BlockSpec((tk, P_COLS + A_COLS), lambda i: (i, 0))],
        out_specs=[pl.BlockSpec((tk, P_COLS), lambda i: (i, 0)),
                   pl.BlockSpec((tk, A_COLS), lambda i: (i, 0))],
        out_shape=[jax.ShapeDtypeStruct((D_MODEL, P_COLS), BF16),
                   jax.ShapeDtypeStruct((D_MODEL, A_COLS), BF16)],
        compiler_params=_params("parallel"),
        name="cast_in_weights",
    )(w_in)


def _prepare_weights(mix_norm_g, w_in, w_gate2, b_gate2, head_norm_g, conv_mix_w, w_out,
                     ffn_norm_g, w_up, ffn_conv_w, ffn_conv_b, w_down, final_norm_g):
    w_main, w_a = _cast_in_weights(w_in)
    wf = w_gate2[0].reshape(GATE_RANK, GLA_HEADS, GLA_DK).transpose(1, 0, 2)
    wb = w_gate2[1].reshape(GATE_RANK, GLA_HEADS, GLA_DK).transpose(1, 0, 2)
    zero = jnp.zeros_like(wf)
    w2blk = jnp.concatenate([jnp.concatenate([wf, zero], axis=2),
                             jnp.concatenate([zero, wb], axis=2)], axis=1).astype(BF16)
    b2blk = jnp.concatenate([b_gate2[0].reshape(GLA_HEADS, 1, GLA_DK),
                             b_gate2[1].reshape(GLA_HEADS, 1, GLA_DK)], axis=2).astype(F32)
    return dict(
        mix_g=mix_norm_g.reshape(1, D_MODEL), w_main=w_main, w_a=w_a, w2blk=w2blk, b2blk=b2blk,
        head_g=head_norm_g.reshape(1, GLA_DV), conv_mix_w=conv_mix_w,
        w_out=w_out.astype(BF16),
        ffn_g=ffn_norm_g.reshape(1, D_MODEL), w_up=w_up.astype(BF16),
        ffn_conv_w=ffn_conv_w,
        ffn_conv_b=ffn_conv_b.reshape(1, D_FF), w_down=w_down.astype(BF16),
        final_g=final_norm_g.reshape(1, D_MODEL))


def kernel(x_prompt, x_sample, meta_tokens, mix_norm_g, w_in, w_gate2, b_gate2, head_norm_g,
           conv_mix_w, w_out, ffn_norm_g, w_up, ffn_conv_w, ffn_conv_b, w_down, final_norm_g):
    w = _prepare_weights(mix_norm_g[0], w_in[0], w_gate2[0], b_gate2[0], head_norm_g[0],
                         conv_mix_w[0], w_out[0], ffn_norm_g[0], w_up[0], ffn_conv_w[0],
                         ffn_conv_b[0], w_down[0], final_norm_g)
    trunks = (x_prompt, x_sample)
    p_meta16, a_meta16 = _in_proj(meta_tokens, w["mix_g"], w["w_main"], w["w_a"], tm=N_META)

    mixed = []
    for x in trunks:
        bsz, seq, _ = x.shape
        x_main = x.reshape(bsz * seq, D_MODEL)
        p_main, a_main = _in_proj(x_main, w["mix_g"], w["w_main"], w["w_a"], tm=1024)
        o_main, o_meta = _gla(p_main, a_main, p_meta16, a_meta16, w["w2blk"], w["b2blk"],
                              w["head_g"], bsz=bsz, seq=seq, shared_meta=True)
        c_main, c_meta = _mixconv(p_main, p_meta16, w["conv_mix_w"], bsz=bsz, seq=seq,
                                  shared_meta=True)
        mixed.append((x_main, o_main, c_main, o_meta, c_meta))

    n_seq = sum(x.shape[0] for x in trunks)
    _, h2_meta = _out_proj(jnp.concatenate([t[3] for t in mixed], axis=0),
                           jnp.concatenate([t[4] for t in mixed], axis=0),
                           w["w_out"], jnp.tile(meta_tokens, (n_seq, 1)), w["ffn_g"],
                           tm=n_seq * N_META)

    outs = []
    meta_off = 0
    for x, (x_main, o_main, c_main, _, _) in zip(trunks, mixed):
        bsz, seq, _ = x.shape
        x1, h2 = _out_proj(o_main, c_main, w["w_out"], x_main, w["ffn_g"], tm=512)
        act = _up_proj(h2, h2_meta, w["w_up"], w["ffn_conv_w"], w["ffn_conv_b"],
                       seq=seq, meta_off=meta_off, tm=1024)
        y = _down_proj(act, w["w_down"], x1, w["final_g"], tm=512)
        outs.append(y.reshape(bsz, seq, D_MODEL))
        meta_off += bsz
    return tuple(outs)
```

```python
import functools

import numpy as np
import jax
import jax.numpy as jnp
from jax import lax
from jax.experimental import pallas as pl
from jax.experimental.pallas import tpu as pltpu

F32 = jnp.float32
BF16 = jnp.bfloat16

D_MODEL = 4096
N_META = 16
GLA_HEADS = 8
GLA_DK = 128
GLA_DV = 256
GLA_KEY_WIDTH = GLA_HEADS * GLA_DK
GLA_WIDTH = GLA_HEADS * GLA_DV
CONV_WIDTH = 2048
GATE_RANK = 16
GATE_TAU = 16.0
D_FF = 11008
EPS = 1e-6

P_COLS = 2 * GLA_KEY_WIDTH + 2 * GLA_WIDTH + 3 * CONV_WIDTH
A_COLS = 2 * GATE_RANK
Q_OFF, K_OFF, V_OFF, R_OFF = 0, GLA_KEY_WIDTH, 2 * GLA_KEY_WIDTH, 2 * GLA_KEY_WIDTH + GLA_WIDTH
CB_OFF = R_OFF + GLA_WIDTH
CC_OFF = CB_OFF + CONV_WIDTH
CH_OFF = CC_OFF + CONV_WIDTH

CHUNK = 256
LEVELS = 8
F32_TILE_ROWS = 8
BF16_TILE_ROWS = 16
TILE_LEVELS = 3
LOG2_E = 1.4426950408889634
HALO = 16
CONV_COLS = 256
NORM_ROWS = 128
V7X_VMEM_LIMIT = 56 * 1024 * 1024
IN_PROJ_TN = 1024
OUT_PROJ_TN = 512
UP_PROJ_TN = 256
UP_BLOCKS_PER_STEP = 2
DOWN_PROJ_TN = 256


def _params(*sem):
    return pltpu.CompilerParams(dimension_semantics=sem, vmem_limit_bytes=V7X_VMEM_LIMIT)


def _rmsnorm_rows(x, g):
    ms = jnp.mean(x * x, axis=-1, keepdims=True)
    return x * lax.rsqrt(ms + EPS) * g


def _silu(x):
    return x * (1.0 / (1.0 + jnp.exp(-x)))


def _nt_dot(a, b):
    return lax.dot_general(a, b, (((1,), (1,)), ((), ())), preferred_element_type=F32)


def _in_proj_body(x_ref, g_ref, w_ref, wa_ref, p_ref, a_ref, h_scr, *, rows):
    @pl.when(pl.program_id(1) == 0)
    def _():
        def norm_rows(i, carry):
            sl = pl.ds(pl.multiple_of(i * rows, rows), rows)
            h_scr[sl, :] = _rmsnorm_rows(x_ref[sl, :], g_ref[...]).astype(BF16)
            return carry
        lax.fori_loop(0, x_ref.shape[0] // rows, norm_rows, 0)
        a_ref[...] = _nt_dot(h_scr[...], wa_ref[...])

    p_ref[...] = jnp.dot(h_scr[...], w_ref[...], preferred_element_type=F32).astype(BF16)


def _in_proj(x, g, w_main, w_a, *, tm):
    m = x.shape[0]
    tn = IN_PROJ_TN
    return pl.pallas_call(
        functools.partial(_in_proj_body, rows=min(NORM_ROWS, tm)),
        grid=(m // tm, P_COLS // tn),
        in_specs=[pl.BlockSpec((tm, D_MODEL), lambda i, j: (i, 0), pipeline_mode=pl.Buffered(1)),
                  pl.BlockSpec((1, D_MODEL), lambda i, j: (0, 0)),
                  pl.BlockSpec((D_MODEL, tn), lambda i, j: (0, j)),
                  pl.BlockSpec((A_COLS, D_MODEL), lambda i, j: (0, 0))],
        out_specs=[pl.BlockSpec((tm, tn), lambda i, j: (i, j)),
                   pl.BlockSpec((tm, A_COLS), lambda i, j: (i, 0))],
        out_shape=[jax.ShapeDtypeStruct((m, P_COLS), BF16),
                   jax.ShapeDtypeStruct((m, A_COLS), F32)],
        scratch_shapes=[pltpu.VMEM((tm, D_MODEL), BF16)],
        compiler_params=_params("parallel", "arbitrary"),
        name="in_proj",
    )(x, g, w_main, w_a)


def _log2_decay(a, w2, b2):
    z = jnp.dot(a.astype(BF16), w2, preferred_element_type=F32) + b2
    return (jnp.minimum(z, 0.0) - jnp.log(1.0 + jnp.exp(-jnp.abs(z)))) * (LOG2_E / GATE_TAU)


def _chunk_intra(q, k, v, lg, lvd):
    half = CHUNK // 2
    shape = (CHUNK, 2 * GLA_DK)
    row = lax.broadcasted_iota(jnp.int32, shape, 0)
    isb = lax.broadcasted_iota(jnp.int32, shape, 1) >= GLA_DK
    qb = q.astype(BF16)
    q2b = jnp.concatenate([qb, qb], axis=1)
    kb = k.astype(BF16)
    k2b = jnp.concatenate([kb, kb], axis=1)
    zero = jnp.zeros(shape, BF16)
    diag = [jnp.zeros((half, half), F32), jnp.zeros((half, half), F32)]

    def keep_level(lvl, ql, kl):
        for d in range(2):
            rows = slice(d * half, (d + 1) * half)
            diag[d] = jnp.where(lvd == lvl, _nt_dot(ql[rows], kl[rows]), diag[d])

    def masked_level(lvl, e):
        upper = (row & (1 << lvl)) != 0
        qside = jnp.logical_xor(upper, isb)
        keep_level(lvl, jnp.where(qside, q2b * e, zero), k2b * e)
        return upper

    n_tiles = CHUNK // F32_TILE_ROWS
    tiled = (n_tiles, F32_TILE_ROWS, 2 * GLA_DK)
    row_t = lax.broadcasted_iota(jnp.int32, (1,) + tiled[1:], 1)
    isb_t = lax.broadcasted_iota(jnp.int32, (1,) + tiled[1:], 2) >= GLA_DK
    tb = lg.reshape(tiled)
    p = jnp.where(isb_t, 0.0, tb)
    for lvl in range(TILE_LEVELS):
        m = 1 << lvl
        upper_t = (row_t & m) != 0
        e = jnp.exp2(jnp.where(upper_t, p, tb - p)).reshape(shape).astype(BF16)
        masked_level(lvl, e)
        tb_prev = pltpu.roll(tb, m, 1)
        tb_next = pltpu.roll(tb, F32_TILE_ROWS - m, 1)
        p = p + jnp.where(upper_t, tb_prev, 0.0)
        tb = tb + jnp.where(upper_t, tb_prev, tb_next)

    off = [jnp.zeros(tiled[1:], F32)]
    for j in range(n_tiles - 1):
        off.append(off[-1] + tb[j])
    total = off[-1] + tb[n_tiles - 1]
    pre_t = [p[j] + off[j] for j in range(n_tiles)]

    for lvl in range(TILE_LEVELS, LEVELS):
        m = 1 << lvl
        tpb = m // F32_TILE_ROWS
        args = []
        for j in range(n_tiles):
            later_start = off[(j // (2 * tpb) * 2 + 1) * tpb]
            args.append(pre_t[j] - later_start if (j // tpb) % 2 else later_start - pre_t[j])
        e = jnp.exp2(jnp.concatenate(args, axis=0)).astype(BF16)
        if m < BF16_TILE_ROWS:
            masked_level(lvl, e)
        elif m < half:
            zeros_m = jnp.zeros((m, GLA_DK), BF16)
            qf, qr = [], []
            for g in range(CHUNK // (2 * m)):
                lo = slice(2 * m * g, 2 * m * g + m)
                up = slice(2 * m * g + m, 2 * m * (g + 1))
                qf += [zeros_m, qb[up] * e[up, :GLA_DK]]
                qr += [qb[lo] * e[lo, GLA_DK:], zeros_m]
            ql = jnp.concatenate([jnp.concatenate(qf, axis=0), jnp.concatenate(qr, axis=0)], axis=1)
            keep_level(lvl, ql, k2b * e)
        else:
            s_lower = _nt_dot(qb[half:] * e[half:, :GLA_DK], kb[:half] * e[:half, :GLA_DK])
            s_upper = _nt_dot(qb[:half] * e[:half, GLA_DK:], kb[half:] * e[half:, GLA_DK:])
    scores = jnp.concatenate([jnp.concatenate([diag[0], s_upper], axis=1),
                              jnp.concatenate([s_lower, diag[1]], axis=1)], axis=0)

    vb = v.astype(BF16)
    qk = jnp.sum(q * k, axis=-1, keepdims=True)
    o_part = jnp.dot(scores.astype(BF16), vb, preferred_element_type=F32) + qk * v
    pre = jnp.concatenate(pre_t, axis=0)
    rest = jnp.concatenate([total] * n_tiles, axis=0) - pre
    khat = k2b * jnp.exp2(jnp.concatenate([rest[:, :GLA_DK], pre[:, GLA_DK:]], axis=1)).astype(BF16)
    qhat = q2b * jnp.exp2(jnp.concatenate([pre[:, :GLA_DK], rest[:, GLA_DK:]], axis=1)).astype(BF16)
    dstate = lax.dot_general(vb, khat, (((0,), (0,)), ((), ())), preferred_element_type=F32)
    return o_part, qhat, dstate, jnp.exp2(total[0:1, :])


def _gla_body(q_ref, k_ref, v_ref, r_ref, a_ref, qm_ref, km_ref, vm_ref, rm_ref, am_ref,
              w2_ref, b2_ref, hg_ref, lv_ref, o_ref, om_ref,
              opart_scr, qhat_scr, st_scr, dec_scr):
    n_main = q_ref.shape[0] // CHUNK
    n = n_main + 1
    scale = GLA_DK ** -0.5
    w2 = w2_ref[...]
    b2 = b2_ref[...]
    pad = CHUNK - N_META

    def intra(c, q, k, v, lg):
        o_part, qhat, dstate, dec = _chunk_intra(q * scale, k, v, lg, lv_ref[...])
        opart_scr[c] = o_part
        qhat_scr[c] = qhat
        st_scr[c] = dstate
        dec_scr[c] = dec

    def behind_zeros(x):
        return jnp.concatenate([jnp.zeros((pad, x.shape[1]), F32), x.astype(F32)], axis=0)

    intra(0, behind_zeros(qm_ref[...]), behind_zeros(km_ref[...]), behind_zeros(vm_ref[...]),
          behind_zeros(_log2_decay(am_ref[...], w2, b2)))

    def intra_main(c, carry):
        sl = pl.ds(pl.multiple_of(c * CHUNK, CHUNK), CHUNK)
        intra(c + 1, q_ref[sl, :].astype(F32), k_ref[sl, :].astype(F32), v_ref[sl, :].astype(F32),
              _log2_decay(a_ref[sl, :], w2, b2))
        return carry
    lax.fori_loop(0, n_main, intra_main, 0)

    def scan_fwd(c, s):
        d = st_scr[c, :, 0:GLA_DK]
        st_scr[c, :, 0:GLA_DK] = s
        return dec_scr[c][:, 0:GLA_DK] * s + d
    lax.fori_loop(0, n, scan_fwd, jnp.zeros((GLA_DV, GLA_DK), F32))

    def scan_bwd(i, s):
        c = n - 1 - i
        d = st_scr[c, :, GLA_DK:]
        st_scr[c, :, GLA_DK:] = s
        return dec_scr[c][:, GLA_DK:] * s + d
    lax.fori_loop(0, n, scan_bwd, jnp.zeros((GLA_DV, GLA_DK), F32))

    def normed(c):
        o = opart_scr[c] + lax.dot_general(qhat_scr[c], st_scr[c].astype(BF16),
                                           (((1,), (1,)), ((), ())), preferred_element_type=F32)
        return _rmsnorm_rows(o, hg_ref[...])

    om_ref[...] = (normed(0)[pad:, :] * _silu(rm_ref[...].astype(F32))).astype(BF16)

    def finish_main(c, carry):
        sl = pl.ds(pl.multiple_of(c * CHUNK, CHUNK), CHUNK)
        o_ref[sl, :] = (normed(c + 1) * _silu(r_ref[sl, :].astype(F32))).astype(BF16)
        return carry
    lax.fori_loop(0, n_main, finish_main, 0)


def _level_matrix():
    half = CHUNK // 2
    t = np.arange(half)
    x = t[:, None] ^ t[None, :]
    lv = np.full((half, half), -1, np.int32)
    nz = x > 0
    lv[nz] = np.floor(np.log2(x[nz])).astype(np.int32)
    return jnp.asarray(lv)


def _gla(p_main, a_main, p_meta, a_meta, w2blk, b2blk, head_g, *, bsz, seq, shared_meta):
    n = seq // CHUNK + 1
    qo, ko = Q_OFF // GLA_DK, K_OFF // GLA_DK
    vo, ro = V_OFF // GLA_DV, R_OFF // GLA_DV
    mrow = (lambda b: 0) if shared_meta else (lambda b: b)
    main = lambda width, off: pl.BlockSpec((seq, width), lambda b, h: (b, off + h))
    meta = lambda width, off: pl.BlockSpec((N_META, width), lambda b, h: (mrow(b), off + h))
    return pl.pallas_call(
        _gla_body,
        grid=(bsz, GLA_HEADS),
        in_specs=[main(GLA_DK, qo), main(GLA_DK, ko), main(GLA_DV, vo), main(GLA_DV, ro),
                  pl.BlockSpec((seq, A_COLS), lambda b, h: (b, 0)),
                  meta(GLA_DK, qo), meta(GLA_DK, ko), meta(GLA_DV, vo), meta(GLA_DV, ro),
                  pl.BlockSpec((N_META, A_COLS), lambda b, h: (mrow(b), 0)),
                  pl.BlockSpec((None, A_COLS, 2 * GLA_DK), lambda b, h: (h, 0, 0)),
                  pl.BlockSpec((None, 1, 2 * GLA_DK), lambda b, h: (h, 0, 0)),
                  pl.BlockSpec((1, GLA_DV), lambda b, h: (0, 0)),
                  pl.BlockSpec((CHUNK // 2, CHUNK // 2), lambda b, h: (0, 0))],
        out_specs=[pl.BlockSpec((seq, GLA_DV), lambda b, h: (b, h)),
                   pl.BlockSpec((N_META, GLA_DV), lambda b, h: (b, h))],
        out_shape=[jax.ShapeDtypeStruct((bsz * seq, GLA_WIDTH), BF16),
                   jax.ShapeDtypeStruct((bsz * N_META, GLA_WIDTH), BF16)],
        scratch_shapes=[pltpu.VMEM((n, CHUNK, GLA_DV), F32),
                        pltpu.VMEM((n, CHUNK, 2 * GLA_DK), BF16),
                        pltpu.VMEM((n, GLA_DV, 2 * GLA_DK), F32),
                        pltpu.VMEM((n, 1, 2 * GLA_DK), F32)],
        compiler_params=_params("parallel", "parallel"),
        name="gla",
    )(p_main, p_main, p_main, p_main, a_main, p_meta, p_meta, p_meta, p_meta, a_meta,
      w2blk, b2blk, head_g, _level_matrix())


def _conv_rows(y_scr, w, start, rows):
    n = rows + 2 * HALO
    win = y_scr[pl.ds(start - HALO, n), :]
    prev = pltpu.roll(win, 1, 0)[HALO:HALO + rows]
    nxt = pltpu.roll(win, n - 1, 0)[HALO:HALO + rows]
    return prev * w[0:1] + win[HALO:HALO + rows] * w[1:2] + nxt * w[2:3]


def _frame(y_scr, y_meta, seq):
    zeros = jnp.zeros((HALO, y_scr.shape[1]), F32)
    y_scr[0:HALO, :] = zeros
    y_scr[HALO:2 * HALO, :] = y_meta
    y_scr[2 * HALO + seq:3 * HALO + seq, :] = zeros


def _mixconv_body(cb_ref, cc_ref, ch_ref, cbm_ref, ccm_ref, chm_ref, w_ref, o_ref, om_ref, y_scr,
                  *, rows):
    seq = cb_ref.shape[0]
    w = w_ref[...]
    _frame(y_scr, ccm_ref[...].astype(F32) * chm_ref[...].astype(F32), seq)

    def fill(i, carry):
        sl = pl.ds(pl.multiple_of(i * rows, rows), rows)
        dst = pl.ds(pl.multiple_of(2 * HALO + i * rows, HALO), rows)
        y_scr[dst, :] = cc_ref[sl, :].astype(F32) * ch_ref[sl, :].astype(F32)
        return carry
    lax.fori_loop(0, seq // rows, fill, 0)

    om_ref[...] = (cbm_ref[...].astype(F32) * _conv_rows(y_scr, w, HALO, N_META)).astype(BF16)

    def tile(i, carry):
        sl = pl.ds(pl.multiple_of(i * rows, rows), rows)
        start = pl.multiple_of(2 * HALO + i * rows, HALO)
        o_ref[sl, :] = (cb_ref[sl, :].astype(F32) * _conv_rows(y_scr, w, start, rows)).astype(BF16)
        return carry
    lax.fori_loop(0, seq // rows, tile, 0)


def _mixconv(p_main, p_meta, conv_w, *, bsz, seq, shared_meta, rows=256):
    cbo, cco, cho = CB_OFF // CONV_COLS, CC_OFF // CONV_COLS, CH_OFF // CONV_COLS
    mrow = (lambda b: 0) if shared_meta else (lambda b: b)
    main = lambda off: pl.BlockSpec((seq, CONV_COLS), lambda b, j: (b, off + j))
    meta = lambda off: pl.BlockSpec((N_META, CONV_COLS), lambda b, j: (mrow(b), off + j))
    return pl.pallas_call(
        functools.partial(_mixconv_body, rows=rows),
        grid=(bsz, CONV_WIDTH // CONV_COLS),
        in_specs=[main(cbo), main(cco), main(cho), meta(cbo), meta(cco), meta(cho),
                  pl.BlockSpec((3, CONV_COLS), lambda b, j: (0, j))],
        out_specs=[pl.BlockSpec((seq, CONV_COLS), lambda b, j: (b, j)),
                   pl.BlockSpec((N_META, CONV_COLS), lambda b, j: (b, j))],
        out_shape=[jax.ShapeDtypeStruct((bsz * seq, CONV_WIDTH), BF16),
                   jax.ShapeDtypeStruct((bsz * N_META, CONV_WIDTH), BF16)],
        scratch_shapes=[pltpu.VMEM((seq + 3 * HALO, CONV_COLS), F32)],
        compiler_params=_params("parallel", "parallel"),
        name="mixconv",
    )(p_main, p_main, p_main, p_meta, p_meta, p_meta, conv_w)


def _norm_block_rows(src_ref, dst_ref, g_ref, rows):
    def body(i, carry):
        sl = pl.ds(pl.multiple_of(i * rows, rows), rows)
        dst_ref[sl, :] = _rmsnorm_rows(src_ref[sl, :], g_ref[...]).astype(dst_ref.dtype)
        return carry
    lax.fori_loop(0, src_ref.shape[0] // rows, body, 0)


def _out_proj_body(o_ref, c_ref, wo_ref, wc_ref, x_ref, g_ref, x1_ref, h2_ref, *, tn, rows):
    j = pl.program_id(1)
    acc = jnp.dot(o_ref[...], wo_ref[...], preferred_element_type=F32)
    acc = acc + jnp.dot(c_ref[...], wc_ref[...], preferred_element_type=F32)
    x1_ref[:, pl.ds(pl.multiple_of(j * tn, tn), tn)] = x_ref[...] + acc

    @pl.when(j == pl.num_programs(1) - 1)
    def _():
        _norm_block_rows(x1_ref, h2_ref, g_ref, rows)


def _out_proj(o, conv, w_out, x, g, *, tm):
    m = x.shape[0]
    tn = OUT_PROJ_TN
    return pl.pallas_call(
        functools.partial(_out_proj_body, tn=tn, rows=min(NORM_ROWS, tm)),
        grid=(m // tm, D_MODEL // tn),
        in_specs=[pl.BlockSpec((tm, GLA_WIDTH), lambda i, j: (i, 0)),
                  pl.BlockSpec((tm, CONV_WIDTH), lambda i, j: (i, 0)),
                  pl.BlockSpec((GLA_WIDTH, tn), lambda i, j: (0, j)),
                  pl.BlockSpec((CONV_WIDTH, tn), lambda i, j: (1, j)),
                  pl.BlockSpec((tm, tn), lambda i, j: (i, j)),
                  pl.BlockSpec((1, D_MODEL), lambda i, j: (0, 0))],
        out_specs=[pl.BlockSpec((tm, D_MODEL), lambda i, j: (i, 0)),
                   pl.BlockSpec((tm, D_MODEL), lambda i, j: (i, 0))],
        out_shape=[jax.ShapeDtypeStruct((m, D_MODEL), F32),
                   jax.ShapeDtypeStruct((m, D_MODEL), BF16)],
        compiler_params=_params("parallel", "arbitrary"),
        name="out_proj",
    )(o, conv, w_out, w_out, x, g)


def _up_proj_body(h_ref, hp_ref, hn_ref, hm_ref, wu_ref, wg_ref, cw_ref, cb_ref, act_ref, lhs_scr,
                  *, tiles_per_seq, rows):
    tm = h_ref.shape[0]
    i = pl.program_id(0)

    @pl.when(pl.program_id(1) == 0)
    def _():
        first = (i % tiles_per_seq) == 0
        last = (i % tiles_per_seq) == tiles_per_seq - 1

        @pl.when(first)
        def _():
            lhs_scr[0:HALO, :] = hm_ref[...]

        @pl.when(jnp.logical_not(first))
        def _():
            lhs_scr[0:HALO, :] = hp_ref[...]

        @pl.when(last)
        def _():
            lhs_scr[HALO + tm:2 * HALO + tm, :] = jnp.zeros((HALO, D_MODEL), BF16)

        @pl.when(jnp.logical_not(last))
        def _():
            lhs_scr[HALO + tm:2 * HALO + tm, :] = hn_ref[...]

        def copy_rows(r, carry):
            sl = pl.ds(pl.multiple_of(r * rows, rows), rows)
            lhs_scr[pl.ds(pl.multiple_of(HALO + r * rows, HALO), rows), :] = h_ref[sl, :]
            return carry
        lax.fori_loop(0, tm // rows, copy_rows, 0)

    def column_blocks(count):
        n = tm + 2 * HALO
        order = None
        for b in range(count):
            cols = slice(b * UP_PROJ_TN, (b + 1) * UP_PROJ_TN)
            g = jnp.dot(lhs_scr[...], wg_ref[:, cols], preferred_element_type=F32)
            u = jnp.dot(h_ref[...], wu_ref[:, cols], preferred_element_type=F32)
            if order is not None:
                g = g + order
            w = cw_ref[:, cols]
            conv = (pltpu.roll(g, 1, 0)[HALO:HALO + tm] * w[0:1] + g[HALO:HALO + tm] * w[1:2]
                    + pltpu.roll(g, n - 1, 0)[HALO:HALO + tm] * w[2:3])
            act = _silu(conv + cb_ref[:, cols]) * u
            act_ref[:, cols] = act.astype(BF16)
            if b + 1 < count:
                acc = act[0:F32_TILE_ROWS]
                for r in range(1, tm // F32_TILE_ROWS):
                    acc = jnp.maximum(acc, act[r * F32_TILE_ROWS:(r + 1) * F32_TILE_ROWS])
                bits = pltpu.bitcast(acc, jnp.int32)
                gone = lax.shift_right_logical(lax.shift_right_logical(bits, 16), 16)
                order = jnp.max(gone.astype(F32), axis=0, keepdims=True)

    full_steps, left = divmod(D_FF // UP_PROJ_TN, UP_BLOCKS_PER_STEP)
    j = pl.program_id(1)

    @pl.when(j < full_steps)
    def _():
        column_blocks(UP_BLOCKS_PER_STEP)

    if left:
        @pl.when(j == full_steps)
        def _():
            column_blocks(left)


def _up_proj(h_main, h_meta, w_u, w_g, conv_w, conv_b, *, seq, meta_off, tm):
    m = h_main.shape[0]
    tn = UP_PROJ_TN * UP_BLOCKS_PER_STEP
    nblk = pl.cdiv(D_FF, tn)
    tiles_per_seq = seq // tm
    halo_per_tile = tm // HALO
    n_halo = m // HALO
    return pl.pallas_call(
        functools.partial(_up_proj_body, tiles_per_seq=tiles_per_seq, rows=64),
        grid=(m // tm, nblk),
        in_specs=[pl.BlockSpec((tm, D_MODEL), lambda i, j: (i, 0)),
                  pl.BlockSpec((HALO, D_MODEL),
                               lambda i, j: (jnp.maximum(i * halo_per_tile - 1, 0), 0)),
                  pl.BlockSpec((HALO, D_MODEL),
                               lambda i, j: (jnp.minimum((i + 1) * halo_per_tile, n_halo - 1), 0)),
                  pl.BlockSpec((HALO, D_MODEL), lambda i, j: (meta_off + i // tiles_per_seq, 0)),
                  pl.BlockSpec((D_MODEL, tn), lambda i, j: (0, j)),
                  pl.BlockSpec((D_MODEL, tn), lambda i, j: (0, j)),
                  pl.BlockSpec((3, tn), lambda i, j: (0, j)),
                  pl.BlockSpec((1, tn), lambda i, j: (0, j))],
        out_specs=pl.BlockSpec((tm, tn), lambda i, j: (i, j)),
        out_shape=jax.ShapeDtypeStruct((m, D_FF), BF16),
        scratch_shapes=[pltpu.VMEM((tm + 2 * HALO, D_MODEL), BF16)],
        compiler_params=_params("parallel", "arbitrary"),
        name="up_proj",
    )(h_main, h_main, h_main, h_meta, w_u, w_g, conv_w, conv_b)


def _down_proj_body(act_ref, w_ref, x1_ref, g_ref, y_ref, *, tn, rows):
    j = pl.program_id(1)
    y_ref[:, pl.ds(pl.multiple_of(j * tn, tn), tn)] = (
        x1_ref[...] + jnp.dot(act_ref[...], w_ref[...], preferred_element_type=F32))

    @pl.when(j == pl.num_programs(1) - 1)
    def _():
        _norm_block_rows(y_ref, y_ref, g_ref, rows)


def _down_proj(act, w_down, x1, g, *, tm):
    m = act.shape[0]
    tn = DOWN_PROJ_TN
    return pl.pallas_call(
        functools.partial(_down_proj_body, tn=tn, rows=NORM_ROWS),
        grid=(m // tm, D_MODEL // tn),
        in_specs=[pl.BlockSpec((tm, D_FF), lambda i, j: (i, 0)),
                  pl.BlockSpec((D_FF, tn), lambda i, j: (0, j)),
                  pl.BlockSpec((tm, tn), lambda i, j: (i, j)),
                  pl.BlockSpec((1, D_MODEL), lambda i, j: (0, 0))],
        out_specs=pl.BlockSpec((tm, D_MODEL), lambda i, j: (i, 0)),
        out_shape=jax.ShapeDtypeStruct((m, D_MODEL), F32),
        compiler_params=_params("parallel", "arbitrary"),
        name="down_proj",
    )(act, w_down, x1, g)


def _cast_in_weights_body(wt_ref, wat_ref, wm_ref, wa_ref):
    wm_ref[...] = wt_ref[...].T.astype(BF16)

    @pl.when(pl.program_id(0) == 0)
    def _():
        wa_ref[...] = wat_ref[...].astype(BF16)


def _cast_in_weights(w_in_t, *, tn=512):
    a_lo = V_OFF + 2 * GLA_WIDTH
    src_row = lambda c: A_COLS * (c * (tn // A_COLS) + jnp.where(c * tn >= a_lo, 1, 0))
    return pl.pallas_call(
        _cast_in_weights_body,
        grid=(P_COLS // tn,),
        in_specs=[pl.BlockSpec((pl.Element(tn), pl.Element(D_MODEL)), lambda c: (src_row(c), 0)),
                  pl.BlockSpec((A_COLS, D_MODEL), lambda c: (a_lo // A_COLS, 0))],
        out_specs=[pl.BlockSpec((D_MODEL, tn), lambda c: (0, c)),
                   pl.BlockSpec((A_COLS, D_MODEL), lambda c: (0, 0))],
        out_shape=[jax.ShapeDtypeStruct((D_MODEL, P_COLS), BF16),
                   jax.ShapeDtypeStruct((A_COLS, D_MODEL), BF16)],
        compiler_params=_params("arbitrary"),
        name="cast_in_weights",
    )(w_in_t, w_in_t)


def _prepare_weights(mix_norm_g, w_in, w_gate2, b_gate2, head_norm_g, conv_mix_w, w_out,
                     ffn_norm_g, w_up, ffn_conv_w, ffn_conv_b, w_down, final_norm_g):
    w_main, w_a = _cast_in_weights(w_in.T)
    wf = w_gate2[0].reshape(GATE_RANK, GLA_HEADS, GLA_DK).transpose(1, 0, 2)
    wb = w_gate2[1].reshape(GATE_RANK, GLA_HEADS, GLA_DK).transpose(1, 0, 2)
    zero = jnp.zeros_like(wf)
    w2blk = jnp.concatenate([jnp.concatenate([wf, zero], axis=2),
                             jnp.concatenate([zero, wb], axis=2)], axis=1).astype(BF16)
    b2blk = jnp.concatenate([b_gate2[0].reshape(GLA_HEADS, 1, GLA_DK),
                             b_gate2[1].reshape(GLA_HEADS, 1, GLA_DK)], axis=2).astype(F32)
    return dict(
        mix_g=mix_norm_g.reshape(1, D_MODEL), w_main=w_main, w_a=w_a, w2blk=w2blk, b2blk=b2blk,
        head_g=head_norm_g.reshape(1, GLA_DV), conv_mix_w=conv_mix_w,
        w_out=w_out.astype(BF16),
        ffn_g=ffn_norm_g.reshape(1, D_MODEL),
        w_u=w_up[:, :D_FF].astype(BF16), w_g=w_up[:, D_FF:].astype(BF16),
        ffn_conv_w=ffn_conv_w,
        ffn_conv_b=ffn_conv_b.reshape(1, D_FF), w_down=w_down.astype(BF16),
        final_g=final_norm_g.reshape(1, D_MODEL))


def kernel(x_prompt, x_sample, meta_tokens, mix_norm_g, w_in, w_gate2, b_gate2, head_norm_g,
           conv_mix_w, w_out, ffn_norm_g, w_up, ffn_conv_w, ffn_conv_b, w_down, final_norm_g):
    w = _prepare_weights(mix_norm_g[0], w_in[0], w_gate2[0], b_gate2[0], head_norm_g[0],
                         conv_mix_w[0], w_out[0], ffn_norm_g[0], w_up[0], ffn_conv_w[0],
                         ffn_conv_b[0], w_down[0], final_norm_g)
    trunks = (x_prompt, x_sample)
    p_meta16, a_meta16 = _in_proj(meta_tokens, w["mix_g"], w["w_main"], w["w_a"], tm=N_META)

    mixed = []
    for x in trunks:
        bsz, seq, _ = x.shape
        x_main = x.reshape(bsz * seq, D_MODEL)
        p_main, a_main = _in_proj(x_main, w["mix_g"], w["w_main"], w["w_a"], tm=1024)
        o_main, o_meta = _gla(p_main, a_main, p_meta16, a_meta16, w["w2blk"], w["b2blk"],
                              w["head_g"], bsz=bsz, seq=seq, shared_meta=True)
        c_main, c_meta = _mixconv(p_main, p_meta16, w["conv_mix_w"], bsz=bsz, seq=seq,
                                  shared_meta=True)
        mixed.append((x_main, o_main, c_main, o_meta, c_meta))

    n_seq = sum(x.shape[0] for x in trunks)
    _, h2_meta = _out_proj(jnp.concatenate([t[3] for t in mixed], axis=0),
                           jnp.concatenate([t[4] for t in mixed], axis=0),
                           w["w_out"], jnp.tile(meta_tokens, (n_seq, 1)), w["ffn_g"],
                           tm=n_seq * N_META)

    outs = []
    meta_off = 0
    for x, (x_main, o_main, c_main, _, _) in zip(trunks, mixed):
        bsz, seq, _ = x.shape
        x1, h2 = _out_proj(o_main, c_main, w["w_out"], x_main, w["ffn_g"], tm=512)
        act = _up_proj(h2, h2_meta, w["w_u"], w["w_g"], w["ffn_conv_w"], w["ffn_conv_b"],
                       seq=seq, meta_off=meta_off, tm=1024)
        y = _down_proj(act, w["w_down"], x1, w["final_g"], tm=512)
        outs.append(y.reshape(bsz, seq, D_MODEL))
        meta_off += bsz
    return tuple(outs)
```

```python
import functools

import numpy as np
import jax
import jax.numpy as jnp
from jax import lax
from jax.experimental import pallas as pl
from jax.experimental.pallas import tpu as pltpu

F32 = jnp.float32
BF16 = jnp.bfloat16

D_MODEL = 4096
N_META = 16
GLA_HEADS = 8
GLA_DK = 128
GLA_DV = 256
GLA_KEY_WIDTH = GLA_HEADS * GLA_DK
GLA_WIDTH = GLA_HEADS * GLA_DV
CONV_WIDTH = 2048
GATE_RANK = 16
GATE_TAU = 16.0
D_FF = 11008
EPS = 1e-6

P_COLS = 2 * GLA_KEY_WIDTH + 2 * GLA_WIDTH + 3 * CONV_WIDTH
A_COLS = 2 * GATE_RANK
Q_OFF, K_OFF, V_OFF, R_OFF = 0, GLA_KEY_WIDTH, 2 * GLA_KEY_WIDTH, 2 * GLA_KEY_WIDTH + GLA_WIDTH
CB_OFF = R_OFF + GLA_WIDTH
CC_OFF = CB_OFF + CONV_WIDTH
CH_OFF = CC_OFF + CONV_WIDTH

CHUNK = 256
LEVELS = 8
F32_TILE_ROWS = 8
BF16_TILE_ROWS = 16
TILE_LEVELS = 3
LOG2_E = 1.4426950408889634
HALO = 16
CONV_COLS = 256
NORM_ROWS = 128
V7X_VMEM_LIMIT = 56 * 1024 * 1024
IN_PROJ_TN = 1024
OUT_PROJ_TN = 512
UP_PROJ_TN = 256
UP_BLOCKS_PER_STEP = 3
DOWN_PROJ_TN = 256


def _params(*sem):
    return pltpu.CompilerParams(dimension_semantics=sem, vmem_limit_bytes=V7X_VMEM_LIMIT)


def _rmsnorm_rows(x, g):
    ms = jnp.mean(x * x, axis=-1, keepdims=True)
    return x * lax.rsqrt(ms + EPS) * g


def _silu(x):
    return x * (1.0 / (1.0 + jnp.exp(-x)))


def _nt_dot(a, b):
    return lax.dot_general(a, b, (((1,), (1,)), ((), ())), preferred_element_type=F32)


def _in_proj_body(x_ref, g_ref, w_ref, wa_ref, p_ref, a_ref, h_scr, *, rows):
    @pl.when(pl.program_id(1) == 0)
    def _():
        def norm_rows(i, carry):
            sl = pl.ds(pl.multiple_of(i * rows, rows), rows)
            h_scr[sl, :] = _rmsnorm_rows(x_ref[sl, :], g_ref[...]).astype(BF16)
            return carry
        lax.fori_loop(0, x_ref.shape[0] // rows, norm_rows, 0)
        a_ref[...] = _nt_dot(h_scr[...], wa_ref[...])

    p_ref[...] = jnp.dot(h_scr[...], w_ref[...], preferred_element_type=F32).astype(BF16)


def _in_proj(x, g, w_main, w_a, *, tm):
    m = x.shape[0]
    tn = IN_PROJ_TN
    return pl.pallas_call(
        functools.partial(_in_proj_body, rows=min(NORM_ROWS, tm)),
        grid=(m // tm, P_COLS // tn),
        in_specs=[pl.BlockSpec((tm, D_MODEL), lambda i, j: (i, 0), pipeline_mode=pl.Buffered(1)),
                  pl.BlockSpec((1, D_MODEL), lambda i, j: (0, 0)),
                  pl.BlockSpec((D_MODEL, tn), lambda i, j: (0, j)),
                  pl.BlockSpec((A_COLS, D_MODEL), lambda i, j: (0, 0))],
        out_specs=[pl.BlockSpec((tm, tn), lambda i, j: (i, j)),
                   pl.BlockSpec((tm, A_COLS), lambda i, j: (i, 0))],
        out_shape=[jax.ShapeDtypeStruct((m, P_COLS), BF16),
                   jax.ShapeDtypeStruct((m, A_COLS), F32)],
        scratch_shapes=[pltpu.VMEM((tm, D_MODEL), BF16)],
        compiler_params=_params("parallel", "arbitrary"),
        name="in_proj",
    )(x, g, w_main, w_a)


def _log2_decay(a, w2, b2):
    z = jnp.dot(a.astype(BF16), w2, preferred_element_type=F32) + b2
    return (jnp.minimum(z, 0.0) - jnp.log(1.0 + jnp.exp(-jnp.abs(z)))) * (LOG2_E / GATE_TAU)


def _chunk_intra(q, k, v, lg, lvd):
    half = CHUNK // 2
    shape = (CHUNK, 2 * GLA_DK)
    row = lax.broadcasted_iota(jnp.int32, shape, 0)
    isb = lax.broadcasted_iota(jnp.int32, shape, 1) >= GLA_DK
    qb = q.astype(BF16)
    q2b = jnp.concatenate([qb, qb], axis=1)
    kb = k.astype(BF16)
    k2b = jnp.concatenate([kb, kb], axis=1)
    zero = jnp.zeros(shape, BF16)
    diag = [jnp.zeros((half, half), F32), jnp.zeros((half, half), F32)]

    def keep_level(lvl, ql, kl):
        for d in range(2):
            rows = slice(d * half, (d + 1) * half)
            diag[d] = jnp.where(lvd == lvl, _nt_dot(ql[rows], kl[rows]), diag[d])

    def masked_level(lvl, e):
        upper = (row & (1 << lvl)) != 0
        qside = jnp.logical_xor(upper, isb)
        keep_level(lvl, jnp.where(qside, q2b * e, zero), k2b * e)
        return upper

    n_tiles = CHUNK // F32_TILE_ROWS
    tiled = (n_tiles, F32_TILE_ROWS, 2 * GLA_DK)
    row_t = lax.broadcasted_iota(jnp.int32, (1,) + tiled[1:], 1)
    isb_t = lax.broadcasted_iota(jnp.int32, (1,) + tiled[1:], 2) >= GLA_DK
    tb = lg.reshape(tiled)
    p = jnp.where(isb_t, 0.0, tb)
    for lvl in range(TILE_LEVELS):
        m = 1 << lvl
        upper_t = (row_t & m) != 0
        e = jnp.exp2(jnp.where(upper_t, p, tb - p)).reshape(shape).astype(BF16)
        masked_level(lvl, e)
        tb_prev = pltpu.roll(tb, m, 1)
        tb_next = pltpu.roll(tb, F32_TILE_ROWS - m, 1)
        p = p + jnp.where(upper_t, tb_prev, 0.0)
        tb = tb + jnp.where(upper_t, tb_prev, tb_next)

    off = [jnp.zeros(tiled[1:], F32)]
    for j in range(n_tiles - 1):
        off.append(off[-1] + tb[j])
    total = off[-1] + tb[n_tiles - 1]
    pre_t = [p[j] + off[j] for j in range(n_tiles)]

    for lvl in range(TILE_LEVELS, LEVELS):
        m = 1 << lvl
        tpb = m // F32_TILE_ROWS
        args = []
        for j in range(n_tiles):
            later_start = off[(j // (2 * tpb) * 2 + 1) * tpb]
            args.append(pre_t[j] - later_start if (j // tpb) % 2 else later_start - pre_t[j])
        e = jnp.exp2(jnp.concatenate(args, axis=0)).astype(BF16)
        if m < BF16_TILE_ROWS:
            masked_level(lvl, e)
        elif m < half:
            zeros_m = jnp.zeros((m, GLA_DK), BF16)
            qf, qr = [], []
            for g in range(CHUNK // (2 * m)):
                lo = slice(2 * m * g, 2 * m * g + m)
                up = slice(2 * m * g + m, 2 * m * (g + 1))
                qf += [zeros_m, qb[up] * e[up, :GLA_DK]]
                qr += [qb[lo] * e[lo, GLA_DK:], zeros_m]
            ql = jnp.concatenate([jnp.concatenate(qf, axis=0), jnp.concatenate(qr, axis=0)], axis=1)
            keep_level(lvl, ql, k2b * e)
        else:
            s_lower = _nt_dot(qb[half:] * e[half:, :GLA_DK], kb[:half] * e[:half, :GLA_DK])
            s_upper = _nt_dot(qb[:half] * e[:half, GLA_DK:], kb[half:] * e[half:, GLA_DK:])
    scores = jnp.concatenate([jnp.concatenate([diag[0], s_upper], axis=1),
                              jnp.concatenate([s_lower, diag[1]], axis=1)], axis=0)

    vb = v.astype(BF16)
    qk = jnp.sum(q * k, axis=-1, keepdims=True)
    o_part = jnp.dot(scores.astype(BF16), vb, preferred_element_type=F32) + qk * v
    pre = jnp.concatenate(pre_t, axis=0)
    rest = jnp.concatenate([total] * n_tiles, axis=0) - pre
    khat = k2b * jnp.exp2(jnp.concatenate([rest[:, :GLA_DK], pre[:, GLA_DK:]], axis=1)).astype(BF16)
    qhat = q2b * jnp.exp2(jnp.concatenate([pre[:, :GLA_DK], rest[:, GLA_DK:]], axis=1)).astype(BF16)
    dstate = lax.dot_general(vb, khat, (((0,), (0,)), ((), ())), preferred_element_type=F32)
    return o_part, qhat, dstate, jnp.exp2(total[0:1, :])


def _gla_body(q_ref, k_ref, v_ref, r_ref, a_ref, qm_ref, km_ref, vm_ref, rm_ref, am_ref,
              w2_ref, b2_ref, hg_ref, lv_ref, o_ref, om_ref,
              opart_scr, qhat_scr, st_scr, dec_scr):
    n_main = q_ref.shape[0] // CHUNK
    n = n_main + 1
    scale = GLA_DK ** -0.5
    w2 = w2_ref[...]
    b2 = b2_ref[...]
    pad = CHUNK - N_META

    def intra(c, q, k, v, lg):
        o_part, qhat, dstate, dec = _chunk_intra(q * scale, k, v, lg, lv_ref[...])
        opart_scr[c] = o_part
        qhat_scr[c] = qhat
        st_scr[c] = dstate
        dec_scr[c] = dec

    def behind_zeros(x):
        return jnp.concatenate([jnp.zeros((pad, x.shape[1]), F32), x.astype(F32)], axis=0)

    intra(0, behind_zeros(qm_ref[...]), behind_zeros(km_ref[...]), behind_zeros(vm_ref[...]),
          behind_zeros(_log2_decay(am_ref[...], w2, b2)))

    def intra_main(c, carry):
        sl = pl.ds(pl.multiple_of(c * CHUNK, CHUNK), CHUNK)
        intra(c + 1, q_ref[sl, :].astype(F32), k_ref[sl, :].astype(F32), v_ref[sl, :].astype(F32),
              _log2_decay(a_ref[sl, :], w2, b2))
        return carry
    lax.fori_loop(0, n_main, intra_main, 0)

    def scan_fwd(c, s):
        d = st_scr[c, :, 0:GLA_DK]
        st_scr[c, :, 0:GLA_DK] = s
        return dec_scr[c][:, 0:GLA_DK] * s + d
    lax.fori_loop(0, n, scan_fwd, jnp.zeros((GLA_DV, GLA_DK), F32))

    def scan_bwd(i, s):
        c = n - 1 - i
        d = st_scr[c, :, GLA_DK:]
        st_scr[c, :, GLA_DK:] = s
        return dec_scr[c][:, GLA_DK:] * s + d
    lax.fori_loop(0, n, scan_bwd, jnp.zeros((GLA_DV, GLA_DK), F32))

    def normed(c):
        o = opart_scr[c] + lax.dot_general(qhat_scr[c], st_scr[c].astype(BF16),
                                           (((1,), (1,)), ((), ())), preferred_element_type=F32)
        return _rmsnorm_rows(o, hg_ref[...])

    om_ref[...] = (normed(0)[pad:, :] * _silu(rm_ref[...].astype(F32))).astype(BF16)

    def finish_main(c, carry):
        sl = pl.ds(pl.multiple_of(c * CHUNK, CHUNK), CHUNK)
        o_ref[sl, :] = (normed(c + 1) * _silu(r_ref[sl, :].astype(F32))).astype(BF16)
        return carry
    lax.fori_loop(0, n_main, finish_main, 0)


def _level_matrix():
    half = CHUNK // 2
    t = np.arange(half)
    x = t[:, None] ^ t[None, :]
    lv = np.full((half, half), -1, np.int32)
    nz = x > 0
    lv[nz] = np.floor(np.log2(x[nz])).astype(np.int32)
    return jnp.asarray(lv)


def _gla(p_main, a_main, p_meta, a_meta, w2blk, b2blk, head_g, *, bsz, seq, shared_meta):
    n = seq // CHUNK + 1
    qo, ko = Q_OFF // GLA_DK, K_OFF // GLA_DK
    vo, ro = V_OFF // GLA_DV, R_OFF // GLA_DV
    mrow = (lambda b: 0) if shared_meta else (lambda b: b)
    main = lambda width, off: pl.BlockSpec((seq, width), lambda b, h: (b, off + h))
    meta = lambda width, off: pl.BlockSpec((N_META, width), lambda b, h: (mrow(b), off + h))
    return pl.pallas_call(
        _gla_body,
        grid=(bsz, GLA_HEADS),
        in_specs=[main(GLA_DK, qo), main(GLA_DK, ko), main(GLA_DV, vo), main(GLA_DV, ro),
                  pl.BlockSpec((seq, A_COLS), lambda b, h: (b, 0)),
                  meta(GLA_DK, qo), meta(GLA_DK, ko), meta(GLA_DV, vo), meta(GLA_DV, ro),
                  pl.BlockSpec((N_META, A_COLS), lambda b, h: (mrow(b), 0)),
                  pl.BlockSpec((None, A_COLS, 2 * GLA_DK), lambda b, h: (h, 0, 0)),
                  pl.BlockSpec((None, 1, 2 * GLA_DK), lambda b, h: (h, 0, 0)),
                  pl.BlockSpec((1, GLA_DV), lambda b, h: (0, 0)),
                  pl.BlockSpec((CHUNK // 2, CHUNK // 2), lambda b, h: (0, 0))],
        out_specs=[pl.BlockSpec((seq, GLA_DV), lambda b, h: (b, h)),
                   pl.BlockSpec((N_META, GLA_DV), lambda b, h: (b, h))],
        out_shape=[jax.ShapeDtypeStruct((bsz * seq, GLA_WIDTH), BF16),
                   jax.ShapeDtypeStruct((bsz * N_META, GLA_WIDTH), BF16)],
        scratch_shapes=[pltpu.VMEM((n, CHUNK, GLA_DV), F32),
                        pltpu.VMEM((n, CHUNK, 2 * GLA_DK), BF16),
                        pltpu.VMEM((n, GLA_DV, 2 * GLA_DK), F32),
                        pltpu.VMEM((n, 1, 2 * GLA_DK), F32)],
        compiler_params=_params("parallel", "parallel"),
        name="gla",
    )(p_main, p_main, p_main, p_main, a_main, p_meta, p_meta, p_meta, p_meta, a_meta,
      w2blk, b2blk, head_g, _level_matrix())


def _conv_rows(y_scr, w, start, rows):
    n = rows + 2 * HALO
    win = y_scr[pl.ds(start - HALO, n), :]
    prev = pltpu.roll(win, 1, 0)[HALO:HALO + rows]
    nxt = pltpu.roll(win, n - 1, 0)[HALO:HALO + rows]
    return prev * w[0:1] + win[HALO:HALO + rows] * w[1:2] + nxt * w[2:3]


def _frame(y_scr, y_meta, seq):
    zeros = jnp.zeros((HALO, y_scr.shape[1]), F32)
    y_scr[0:HALO, :] = zeros
    y_scr[HALO:2 * HALO, :] = y_meta
    y_scr[2 * HALO + seq:3 * HALO + seq, :] = zeros


def _mixconv_body(cb_ref, cc_ref, ch_ref, cbm_ref, ccm_ref, chm_ref, w_ref, o_ref, om_ref, y_scr,
                  *, rows):
    seq = cb_ref.shape[0]
    w = w_ref[...]
    _frame(y_scr, ccm_ref[...].astype(F32) * chm_ref[...].astype(F32), seq)

    def fill(i, carry):
        sl = pl.ds(pl.multiple_of(i * rows, rows), rows)
        dst = pl.ds(pl.multiple_of(2 * HALO + i * rows, HALO), rows)
        y_scr[dst, :] = cc_ref[sl, :].astype(F32) * ch_ref[sl, :].astype(F32)
        return carry
    lax.fori_loop(0, seq // rows, fill, 0)

    om_ref[...] = (cbm_ref[...].astype(F32) * _conv_rows(y_scr, w, HALO, N_META)).astype(BF16)

    def tile(i, carry):
        sl = pl.ds(pl.multiple_of(i * rows, rows), rows)
        start = pl.multiple_of(2 * HALO + i * rows, HALO)
        o_ref[sl, :] = (cb_ref[sl, :].astype(F32) * _conv_rows(y_scr, w, start, rows)).astype(BF16)
        return carry
    lax.fori_loop(0, seq // rows, tile, 0)


def _mixconv(p_main, p_meta, conv_w, *, bsz, seq, shared_meta, rows=256):
    cbo, cco, cho = CB_OFF // CONV_COLS, CC_OFF // CONV_COLS, CH_OFF // CONV_COLS
    mrow = (lambda b: 0) if shared_meta else (lambda b: b)
    main = lambda off: pl.BlockSpec((seq, CONV_COLS), lambda b, j: (b, off + j))
    meta = lambda off: pl.BlockSpec((N_META, CONV_COLS), lambda b, j: (mrow(b), off + j))
    return pl.pallas_call(
        functools.partial(_mixconv_body, rows=rows),
        grid=(bsz, CONV_WIDTH // CONV_COLS),
        in_specs=[main(cbo), main(cco), main(cho), meta(cbo), meta(cco), meta(cho),
                  pl.BlockSpec((3, CONV_COLS), lambda b, j: (0, j))],
        out_specs=[pl.BlockSpec((seq, CONV_COLS), lambda b, j: (b, j)),
                   pl.BlockSpec((N_META, CONV_COLS), lambda b, j: (b, j))],
        out_shape=[jax.ShapeDtypeStruct((bsz * seq, CONV_WIDTH), BF16),
                   jax.ShapeDtypeStruct((bsz * N_META, CONV_WIDTH), BF16)],
        scratch_shapes=[pltpu.VMEM((seq + 3 * HALO, CONV_COLS), F32)],
        compiler_params=_params("parallel", "parallel"),
        name="mixconv",
    )(p_main, p_main, p_main, p_meta, p_meta, p_meta, conv_w)


def _norm_block_rows(src_ref, dst_ref, g_ref, rows):
    def body(i, carry):
        sl = pl.ds(pl.multiple_of(i * rows, rows), rows)
        dst_ref[sl, :] = _rmsnorm_rows(src_ref[sl, :], g_ref[...]).astype(dst_ref.dtype)
        return carry
    lax.fori_loop(0, src_ref.shape[0] // rows, body, 0)


def _out_proj_body(o_ref, c_ref, wo_ref, wc_ref, x_ref, g_ref, x1_ref, h2_ref, *, tn, rows):
    j = pl.program_id(1)
    acc = jnp.dot(o_ref[...], wo_ref[...], preferred_element_type=F32)
    acc = acc + jnp.dot(c_ref[...], wc_ref[...], preferred_element_type=F32)
    x1_ref[:, pl.ds(pl.multiple_of(j * tn, tn), tn)] = x_ref[...] + acc

    @pl.when(j == pl.num_programs(1) - 1)
    def _():
        _norm_block_rows(x1_ref, h2_ref, g_ref, rows)


def _out_proj(o, conv, w_out, x, g, *, tm):
    m = x.shape[0]
    tn = OUT_PROJ_TN
    return pl.pallas_call(
        functools.partial(_out_proj_body, tn=tn, rows=min(NORM_ROWS, tm)),
        grid=(m // tm, D_MODEL // tn),
        in_specs=[pl.BlockSpec((tm, GLA_WIDTH), lambda i, j: (i, 0)),
                  pl.BlockSpec((tm, CONV_WIDTH), lambda i, j: (i, 0)),
                  pl.BlockSpec((GLA_WIDTH, tn), lambda i, j: (0, j)),
                  pl.BlockSpec((CONV_WIDTH, tn), lambda i, j: (1, j)),
                  pl.BlockSpec((tm, tn), lambda i, j: (i, j)),
                  pl.BlockSpec((1, D_MODEL), lambda i, j: (0, 0))],
        out_specs=[pl.BlockSpec((tm, D_MODEL), lambda i, j: (i, 0)),
                   pl.BlockSpec((tm, D_MODEL), lambda i, j: (i, 0))],
        out_shape=[jax.ShapeDtypeStruct((m, D_MODEL), F32),
                   jax.ShapeDtypeStruct((m, D_MODEL), BF16)],
        compiler_params=_params("parallel", "arbitrary"),
        name="out_proj",
    )(o, conv, w_out, w_out, x, g)


def _up_proj_body(h_ref, hp_ref, hn_ref, hm_ref, wu_ref, wg_ref, cw_ref, cb_ref, act_ref, lhs_scr,
                  *, tiles_per_seq, rows):
    tm = h_ref.shape[0]
    i = pl.program_id(0)

    @pl.when(pl.program_id(1) == 0)
    def _():
        first = (i % tiles_per_seq) == 0
        last = (i % tiles_per_seq) == tiles_per_seq - 1

        @pl.when(first)
        def _():
            lhs_scr[0:HALO, :] = hm_ref[...]

        @pl.when(jnp.logical_not(first))
        def _():
            lhs_scr[0:HALO, :] = hp_ref[...]

        @pl.when(last)
        def _():
            lhs_scr[HALO + tm:2 * HALO + tm, :] = jnp.zeros((HALO, D_MODEL), BF16)

        @pl.when(jnp.logical_not(last))
        def _():
            lhs_scr[HALO + tm:2 * HALO + tm, :] = hn_ref[...]

        def copy_rows(r, carry):
            sl = pl.ds(pl.multiple_of(r * rows, rows), rows)
            lhs_scr[pl.ds(pl.multiple_of(HALO + r * rows, HALO), rows), :] = h_ref[sl, :]
            return carry
        lax.fori_loop(0, tm // rows, copy_rows, 0)

    def column_blocks(count):
        n = tm + 2 * HALO
        order = None
        w_shift = (UP_BLOCKS_PER_STEP - count) * UP_PROJ_TN
        for b in range(count):
            cols = slice(b * UP_PROJ_TN, (b + 1) * UP_PROJ_TN)
            w_cols = slice(w_shift + b * UP_PROJ_TN, w_shift + (b + 1) * UP_PROJ_TN)
            g = jnp.dot(lhs_scr[...], wg_ref[:, w_cols], preferred_element_type=F32)
            u = jnp.dot(h_ref[...], wu_ref[:, w_cols], preferred_element_type=F32)
            if order is not None:
                g = g + order
            w = cw_ref[:, cols]
            conv = (pltpu.roll(g, 1, 0)[HALO:HALO + tm] * w[0:1] + g[HALO:HALO + tm] * w[1:2]
                    + pltpu.roll(g, n - 1, 0)[HALO:HALO + tm] * w[2:3])
            act = _silu(conv + cb_ref[:, cols]) * u
            act_ref[:, cols] = act.astype(BF16)
            if b + 1 < count:
                acc = act[0:F32_TILE_ROWS]
                for r in range(1, tm // F32_TILE_ROWS):
                    acc = jnp.maximum(acc, act[r * F32_TILE_ROWS:(r + 1) * F32_TILE_ROWS])
                bits = pltpu.bitcast(acc, jnp.int32)
                gone = lax.shift_right_logical(lax.shift_right_logical(bits, 16), 16)
                order = jnp.max(gone.astype(F32), axis=0, keepdims=True)

    full_steps, left = divmod(D_FF // UP_PROJ_TN, UP_BLOCKS_PER_STEP)
    j = pl.program_id(1)

    @pl.when(j < full_steps)
    def _():
        column_blocks(UP_BLOCKS_PER_STEP)

    if left:
        @pl.when(j == full_steps)
        def _():
            column_blocks(left)


def _up_proj(h_main, h_meta, w_up, conv_w, conv_b, *, seq, meta_off, tm):
    m = h_main.shape[0]
    tn = UP_PROJ_TN * UP_BLOCKS_PER_STEP
    nblk = pl.cdiv(D_FF, tn)
    tiles_per_seq = seq // tm
    halo_per_tile = tm // HALO
    n_halo = m // HALO
    blocks_per_half = D_FF // UP_PROJ_TN

    def w_window(half):
        def index(i, j):
            first = jnp.minimum(j * UP_BLOCKS_PER_STEP, blocks_per_half - UP_BLOCKS_PER_STEP)
            return 0, UP_PROJ_TN * (half * blocks_per_half + first)
        return pl.BlockSpec((pl.Element(D_MODEL), pl.Element(tn)), index)

    return pl.pallas_call(
        functools.partial(_up_proj_body, tiles_per_seq=tiles_per_seq, rows=64),
        grid=(m // tm, nblk),
        in_specs=[pl.BlockSpec((tm, D_MODEL), lambda i, j: (i, 0)),
                  pl.BlockSpec((HALO, D_MODEL),
                               lambda i, j: (jnp.maximum(i * halo_per_tile - 1, 0), 0)),
                  pl.BlockSpec((HALO, D_MODEL),
                               lambda i, j: (jnp.minimum((i + 1) * halo_per_tile, n_halo - 1), 0)),
                  pl.BlockSpec((HALO, D_MODEL), lambda i, j: (meta_off + i // tiles_per_seq, 0)),
                  w_window(0),
                  w_window(1),
                  pl.BlockSpec((3, tn), lambda i, j: (0, j)),
                  pl.BlockSpec((1, tn), lambda i, j: (0, j))],
        out_specs=pl.BlockSpec((tm, tn), lambda i, j: (i, j)),
        out_shape=jax.ShapeDtypeStruct((m, D_FF), BF16),
        scratch_shapes=[pltpu.VMEM((tm + 2 * HALO, D_MODEL), BF16)],
        compiler_params=_params("parallel", "arbitrary"),
        name="up_proj",
    )(h_main, h_main, h_main, h_meta, w_up, w_up, conv_w, conv_b)


def _down_proj_body(act_ref, w_ref, x1_ref, g_ref, y_ref, *, tn, rows):
    j = pl.program_id(1)
    y_ref[:, pl.ds(pl.multiple_of(j * tn, tn), tn)] = (
        x1_ref[...] + jnp.dot(act_ref[...], w_ref[...], preferred_element_type=F32))

    @pl.when(j == pl.num_programs(1) - 1)
    def _():
        _norm_block_rows(y_ref, y_ref, g_ref, rows)


def _down_proj(act, w_down, x1, g, *, tm):
    m = act.shape[0]
    tn = DOWN_PROJ_TN
    return pl.pallas_call(
        functools.partial(_down_proj_body, tn=tn, rows=NORM_ROWS),
        grid=(m // tm, D_MODEL // tn),
        in_specs=[pl.BlockSpec((tm, D_FF), lambda i, j: (i, 0)),
                  pl.BlockSpec((D_FF, tn), lambda i, j: (0, j)),
                  pl.BlockSpec((tm, tn), lambda i, j: (i, j)),
                  pl.BlockSpec((1, D_MODEL), lambda i, j: (0, 0))],
        out_specs=pl.BlockSpec((tm, D_MODEL), lambda i, j: (i, 0)),
        out_shape=jax.ShapeDtypeStruct((m, D_MODEL), F32),
        compiler_params=_params("parallel", "arbitrary"),
        name="down_proj",
    )(act, w_down, x1, g)


def _cast_in_weights_body(wt_ref, wat_ref, wm_ref, wa_ref):
    wm_ref[...] = wt_ref[...].T.astype(BF16)

    @pl.when(pl.program_id(0) == 0)
    def _():
        wa_ref[...] = wat_ref[...].astype(BF16)


def _cast_in_weights(w_in_t, *, tn=512):
    a_lo = V_OFF + 2 * GLA_WIDTH
    src_row = lambda c: A_COLS * (c * (tn // A_COLS) + jnp.where(c * tn >= a_lo, 1, 0))
    return pl.pallas_call(
        _cast_in_weights_body,
        grid=(P_COLS // tn,),
        in_specs=[pl.BlockSpec((pl.Element(tn), pl.Element(D_MODEL)), lambda c: (src_row(c), 0)),
                  pl.BlockSpec((A_COLS, D_MODEL), lambda c: (a_lo // A_COLS, 0))],
        out_specs=[pl.BlockSpec((D_MODEL, tn), lambda c: (0, c)),
                   pl.BlockSpec((A_COLS, D_MODEL), lambda c: (0, 0))],
        out_shape=[jax.ShapeDtypeStruct((D_MODEL, P_COLS), BF16),
                   jax.ShapeDtypeStruct((A_COLS, D_MODEL), BF16)],
        compiler_params=_params("arbitrary"),
        name="cast_in_weights",
    )(w_in_t, w_in_t)


def _prepare_weights(mix_norm_g, w_in, w_gate2, b_gate2, head_norm_g, conv_mix_w, w_out,
                     ffn_norm_g, w_up, ffn_conv_w, ffn_conv_b, w_down, final_norm_g):
    w_main, w_a = _cast_in_weights(w_in.T)
    wf = w_gate2[0].reshape(GATE_RANK, GLA_HEADS, GLA_DK).transpose(1, 0, 2)
    wb = w_gate2[1].reshape(GATE_RANK, GLA_HEADS, GLA_DK).transpose(1, 0, 2)
    zero = jnp.zeros_like(wf)
    w2blk = jnp.concatenate([jnp.concatenate([wf, zero], axis=2),
                             jnp.concatenate([zero, wb], axis=2)], axis=1).astype(BF16)
    b2blk = jnp.concatenate([b_gate2[0].reshape(GLA_HEADS, 1, GLA_DK),
                             b_gate2[1].reshape(GLA_HEADS, 1, GLA_DK)], axis=2).astype(F32)
    return dict(
        mix_g=mix_norm_g.reshape(1, D_MODEL), w_main=w_main, w_a=w_a, w2blk=w2blk, b2blk=b2blk,
        head_g=head_norm_g.reshape(1, GLA_DV), conv_mix_w=conv_mix_w,
        w_out=w_out.astype(BF16),
        ffn_g=ffn_norm_g.reshape(1, D_MODEL), w_up=w_up.astype(BF16),
        ffn_conv_w=ffn_conv_w,
        ffn_conv_b=ffn_conv_b.reshape(1, D_FF), w_down=w_down.astype(BF16),
        final_g=final_norm_g.reshape(1, D_MODEL))


def kernel(x_prompt, x_sample, meta_tokens, mix_norm_g, w_in, w_gate2, b_gate2, head_norm_g,
           conv_mix_w, w_out, ffn_norm_g, w_up, ffn_conv_w, ffn_conv_b, w_down, final_norm_g):
    w = _prepare_weights(mix_norm_g[0], w_in[0], w_gate2[0], b_gate2[0], head_norm_g[0],
                         conv_mix_w[0], w_out[0], ffn_norm_g[0], w_up[0], ffn_conv_w[0],
                         ffn_conv_b[0], w_down[0], final_norm_g)
    trunks = (x_prompt, x_sample)
    p_meta16, a_meta16 = _in_proj(meta_tokens, w["mix_g"], w["w_main"], w["w_a"], tm=N_META)

    mixed = []
    for x in trunks:
        bsz, seq, _ = x.shape
        x_main = x.reshape(bsz * seq, D_MODEL)
        p_main, a_main = _in_proj(x_main, w["mix_g"], w["w_main"], w["w_a"], tm=1024)
        o_main, o_meta = _gla(p_main, a_main, p_meta16, a_meta16, w["w2blk"], w["b2blk"],
                              w["head_g"], bsz=bsz, seq=seq, shared_meta=True)
        c_main, c_meta = _mixconv(p_main, p_meta16, w["conv_mix_w"], bsz=bsz, seq=seq,
                                  shared_meta=True)
        mixed.append((x_main, o_main, c_main, o_meta, c_meta))

    n_seq = sum(x.shape[0] for x in trunks)
    _, h2_meta = _out_proj(jnp.concatenate([t[3] for t in mixed], axis=0),
                           jnp.concatenate([t[4] for t in mixed], axis=0),
                           w["w_out"], jnp.tile(meta_tokens, (n_seq, 1)), w["ffn_g"],
                           tm=n_seq * N_META)

    outs = []
    meta_off = 0
    for x, (x_main, o_main, c_main, _, _) in zip(trunks, mixed):
        bsz, seq, _ = x.shape
        x1, h2 = _out_proj(o_main, c_main, w["w_out"], x_main, w["ffn_g"], tm=512)
        act = _up_proj(h2, h2_meta, w["w_up"], w["ffn_conv_w"], w["ffn_conv_b"],
                       seq=seq, meta_off=meta_off, tm=1024)
        y = _down_proj(act, w["w_down"], x1, w["final_g"], tm=512)
        outs.append(y.reshape(bsz, seq, D_MODEL))
        meta_off += bsz
    return tuple(outs)
```

```python
import functools

import numpy as np
import jax
import jax.numpy as jnp
from jax import lax
from jax.experimental import pallas as pl
from jax.experimental.pallas import tpu as pltpu

F32 = jnp.float32
BF16 = jnp.bfloat16

D_MODEL = 4096
N_META = 16
GLA_HEADS = 8
GLA_DK = 128
GLA_DV = 256
GLA_KEY_WIDTH = GLA_HEADS * GLA_DK
GLA_WIDTH = GLA_HEADS * GLA_DV
CONV_WIDTH = 2048
GATE_RANK = 16
GATE_TAU = 16.0
D_FF = 11008
EPS = 1e-6

P_COLS = 2 * GLA_KEY_WIDTH + 2 * GLA_WIDTH + 3 * CONV_WIDTH
A_COLS = 2 * GATE_RANK
Q_OFF, K_OFF, V_OFF, R_OFF = 0, GLA_KEY_WIDTH, 2 * GLA_KEY_WIDTH, 2 * GLA_KEY_WIDTH + GLA_WIDTH
CB_OFF = R_OFF + GLA_WIDTH
CC_OFF = CB_OFF + CONV_WIDTH
CH_OFF = CC_OFF + CONV_WIDTH

CHUNK = 256
LEVELS = 8
F32_TILE_ROWS = 8
BF16_TILE_ROWS = 16
TILE_LEVELS = 3
LOG2_E = 1.4426950408889634
HALO = 16
CONV_COLS = 256
NORM_ROWS = 128
V7X_VMEM_LIMIT = 56 * 1024 * 1024
IN_PROJ_TN = 1024
OUT_PROJ_TN = 1024
UP_PROJ_TN = 256
UP_BLOCKS_PER_STEP = 3
DOWN_PROJ_TN = 256


def _params(*sem):
    return pltpu.CompilerParams(dimension_semantics=sem, vmem_limit_bytes=V7X_VMEM_LIMIT)


def _rmsnorm_rows(x, g):
    ms = jnp.mean(x * x, axis=-1, keepdims=True)
    return x * lax.rsqrt(ms + EPS) * g


def _silu(x):
    return x * (1.0 / (1.0 + jnp.exp(-x)))


def _nt_dot(a, b):
    return lax.dot_general(a, b, (((1,), (1,)), ((), ())), preferred_element_type=F32)


def _in_proj_body(x_ref, g_ref, w_ref, wa_ref, p_ref, a_ref, h_scr, *, rows):
    @pl.when(pl.program_id(1) == 0)
    def _():
        def norm_rows(i, carry):
            sl = pl.ds(pl.multiple_of(i * rows, rows), rows)
            h_scr[sl, :] = _rmsnorm_rows(x_ref[sl, :], g_ref[...]).astype(BF16)
            return carry
        lax.fori_loop(0, x_ref.shape[0] // rows, norm_rows, 0)
        a_ref[...] = _nt_dot(h_scr[...], wa_ref[...])

    p_ref[...] = jnp.dot(h_scr[...], w_ref[...], preferred_element_type=F32).astype(BF16)


def _in_proj(x, g, w_main, w_a, *, tm):
    m = x.shape[0]
    tn = IN_PROJ_TN
    return pl.pallas_call(
        functools.partial(_in_proj_body, rows=min(NORM_ROWS, tm)),
        grid=(m // tm, P_COLS // tn),
        in_specs=[pl.BlockSpec((tm, D_MODEL), lambda i, j: (i, 0), pipeline_mode=pl.Buffered(1)),
                  pl.BlockSpec((1, D_MODEL), lambda i, j: (0, 0)),
                  pl.BlockSpec((D_MODEL, tn), lambda i, j: (0, j)),
                  pl.BlockSpec((A_COLS, D_MODEL), lambda i, j: (0, 0))],
        out_specs=[pl.BlockSpec((tm, tn), lambda i, j: (i, j)),
                   pl.BlockSpec((tm, A_COLS), lambda i, j: (i, 0))],
        out_shape=[jax.ShapeDtypeStruct((m, P_COLS), BF16),
                   jax.ShapeDtypeStruct((m, A_COLS), F32)],
        scratch_shapes=[pltpu.VMEM((tm, D_MODEL), BF16)],
        compiler_params=_params("parallel", "arbitrary"),
        name="in_proj",
    )(x, g, w_main, w_a)


def _log2_decay(a, w2, b2):
    z = jnp.dot(a.astype(BF16), w2, preferred_element_type=F32) + b2
    return (jnp.minimum(z, 0.0) - jnp.log(1.0 + jnp.exp(-jnp.abs(z)))) * (LOG2_E / GATE_TAU)


def _chunk_intra(q, k, v, lg, lvd):
    half = CHUNK // 2
    shape = (CHUNK, 2 * GLA_DK)
    row = lax.broadcasted_iota(jnp.int32, shape, 0)
    isb = lax.broadcasted_iota(jnp.int32, shape, 1) >= GLA_DK
    qb = q.astype(BF16)
    q2b = jnp.concatenate([qb, qb], axis=1)
    kb = k.astype(BF16)
    k2b = jnp.concatenate([kb, kb], axis=1)
    zero = jnp.zeros(shape, BF16)
    diag = [jnp.zeros((half, half), F32), jnp.zeros((half, half), F32)]

    def keep_level(lvl, ql, kl):
        for d in range(2):
            rows = slice(d * half, (d + 1) * half)
            diag[d] = jnp.where(lvd == lvl, _nt_dot(ql[rows], kl[rows]), diag[d])

    def masked_level(lvl, e):
        upper = (row & (1 << lvl)) != 0
        qside = jnp.logical_xor(upper, isb)
        keep_level(lvl, jnp.where(qside, q2b * e, zero), k2b * e)
        return upper

    n_tiles = CHUNK // F32_TILE_ROWS
    tiled = (n_tiles, F32_TILE_ROWS, 2 * GLA_DK)
    row_t = lax.broadcasted_iota(jnp.int32, (1,) + tiled[1:], 1)
    isb_t = lax.broadcasted_iota(jnp.int32, (1,) + tiled[1:], 2) >= GLA_DK
    tb = lg.reshape(tiled)
    p = jnp.where(isb_t, 0.0, tb)
    for lvl in range(TILE_LEVELS):
        m = 1 << lvl
        upper_t = (row_t & m) != 0
        e = jnp.exp2(jnp.where(upper_t, p, tb - p)).reshape(shape).astype(BF16)
        masked_level(lvl, e)
        tb_prev = pltpu.roll(tb, m, 1)
        tb_next = pltpu.roll(tb, F32_TILE_ROWS - m, 1)
        p = p + jnp.where(upper_t, tb_prev, 0.0)
        tb = tb + jnp.where(upper_t, tb_prev, tb_next)

    off = [jnp.zeros(tiled[1:], F32)]
    for j in range(n_tiles - 1):
        off.append(off[-1] + tb[j])
    total = off[-1] + tb[n_tiles - 1]
    pre_t = [p[j] + off[j] for j in range(n_tiles)]

    for lvl in range(TILE_LEVELS, LEVELS):
        m = 1 << lvl
        tpb = m // F32_TILE_ROWS
        args = []
        for j in range(n_tiles):
            later_start = off[(j // (2 * tpb) * 2 + 1) * tpb]
            args.append(pre_t[j] - later_start if (j // tpb) % 2 else later_start - pre_t[j])
        e = jnp.exp2(jnp.concatenate(args, axis=0)).astype(BF16)
        if m < BF16_TILE_ROWS:
            masked_level(lvl, e)
        elif m < half:
            zeros_m = jnp.zeros((m, GLA_DK), BF16)
            qf, qr = [], []
            for g in range(CHUNK // (2 * m)):
                lo = slice(2 * m * g, 2 * m * g + m)
                up = slice(2 * m * g + m, 2 * m * (g + 1))
                qf += [zeros_m, qb[up] * e[up, :GLA_DK]]
                qr += [qb[lo] * e[lo, GLA_DK:], zeros_m]
            ql = jnp.concatenate([jnp.concatenate(qf, axis=0), jnp.concatenate(qr, axis=0)], axis=1)
            keep_level(lvl, ql, k2b * e)
        else:
            s_lower = _nt_dot(qb[half:] * e[half:, :GLA_DK], kb[:half] * e[:half, :GLA_DK])
            s_upper = _nt_dot(qb[:half] * e[:half, GLA_DK:], kb[half:] * e[half:, GLA_DK:])
    scores = jnp.concatenate([jnp.concatenate([diag[0], s_upper], axis=1),
                              jnp.concatenate([s_lower, diag[1]], axis=1)], axis=0)

    vb = v.astype(BF16)
    qk = jnp.sum(q * k, axis=-1, keepdims=True)
    o_part = jnp.dot(scores.astype(BF16), vb, preferred_element_type=F32) + qk * v
    pre = jnp.concatenate(pre_t, axis=0)
    rest = jnp.concatenate([total] * n_tiles, axis=0) - pre
    khat = k2b * jnp.exp2(jnp.concatenate([rest[:, :GLA_DK], pre[:, GLA_DK:]], axis=1)).astype(BF16)
    qhat = q2b * jnp.exp2(jnp.concatenate([pre[:, :GLA_DK], rest[:, GLA_DK:]], axis=1)).astype(BF16)
    dstate = lax.dot_general(vb, khat, (((0,), (0,)), ((), ())), preferred_element_type=F32)
    return o_part, qhat, dstate, jnp.exp2(total[0:1, :])


def _gla_body(q_ref, k_ref, v_ref, r_ref, a_ref, qm_ref, km_ref, vm_ref, rm_ref, am_ref,
              w2_ref, b2_ref, hg_ref, lv_ref, o_ref, om_ref,
              opart_scr, qhat_scr, st_scr, dec_scr):
    n_main = q_ref.shape[0] // CHUNK
    n = n_main + 1
    scale = GLA_DK ** -0.5
    w2 = w2_ref[...]
    b2 = b2_ref[...]
    pad = CHUNK - N_META

    def intra(c, q, k, v, lg):
        o_part, qhat, dstate, dec = _chunk_intra(q * scale, k, v, lg, lv_ref[...])
        opart_scr[c] = o_part
        qhat_scr[c] = qhat
        st_scr[c] = dstate
        dec_scr[c] = dec

    def behind_zeros(x):
        return jnp.concatenate([jnp.zeros((pad, x.shape[1]), F32), x.astype(F32)], axis=0)

    intra(0, behind_zeros(qm_ref[...]), behind_zeros(km_ref[...]), behind_zeros(vm_ref[...]),
          behind_zeros(_log2_decay(am_ref[...], w2, b2)))

    def intra_main(c, carry):
        sl = pl.ds(pl.multiple_of(c * CHUNK, CHUNK), CHUNK)
        intra(c + 1, q_ref[sl, :].astype(F32), k_ref[sl, :].astype(F32), v_ref[sl, :].astype(F32),
              _log2_decay(a_ref[sl, :], w2, b2))
        return carry
    lax.fori_loop(0, n_main, intra_main, 0)

    def scan_fwd(c, s):
        d = st_scr[c, :, 0:GLA_DK]
        st_scr[c, :, 0:GLA_DK] = s
        return dec_scr[c][:, 0:GLA_DK] * s + d
    lax.fori_loop(0, n, scan_fwd, jnp.zeros((GLA_DV, GLA_DK), F32))

    def scan_bwd(i, s):
        c = n - 1 - i
        d = st_scr[c, :, GLA_DK:]
        st_scr[c, :, GLA_DK:] = s
        return dec_scr[c][:, GLA_DK:] * s + d
    lax.fori_loop(0, n, scan_bwd, jnp.zeros((GLA_DV, GLA_DK), F32))

    def normed(c):
        o = opart_scr[c] + lax.dot_general(qhat_scr[c], st_scr[c].astype(BF16),
                                           (((1,), (1,)), ((), ())), preferred_element_type=F32)
        return _rmsnorm_rows(o, hg_ref[...])

    om_ref[...] = (normed(0)[pad:, :] * _silu(rm_ref[...].astype(F32))).astype(BF16)

    def finish_main(c, carry):
        sl = pl.ds(pl.multiple_of(c * CHUNK, CHUNK), CHUNK)
        o_ref[sl, :] = (normed(c + 1) * _silu(r_ref[sl, :].astype(F32))).astype(BF16)
        return carry
    lax.fori_loop(0, n_main, finish_main, 0)


def _level_matrix():
    half = CHUNK // 2
    t = np.arange(half)
    x = t[:, None] ^ t[None, :]
    lv = np.full((half, half), -1, np.int32)
    nz = x > 0
    lv[nz] = np.floor(np.log2(x[nz])).astype(np.int32)
    return jnp.asarray(lv)


def _gla(p_main, a_main, p_meta, a_meta, w2blk, b2blk, head_g, *, bsz, seq, shared_meta):
    n = seq // CHUNK + 1
    qo, ko = Q_OFF // GLA_DK, K_OFF // GLA_DK
    vo, ro = V_OFF // GLA_DV, R_OFF // GLA_DV
    mrow = (lambda b: 0) if shared_meta else (lambda b: b)
    main = lambda width, off: pl.BlockSpec((seq, width), lambda b, h: (b, off + h))
    meta = lambda width, off: pl.BlockSpec((N_META, width), lambda b, h: (mrow(b), off + h))
    return pl.pallas_call(
        _gla_body,
        grid=(bsz, GLA_HEADS),
        in_specs=[main(GLA_DK, qo), main(GLA_DK, ko), main(GLA_DV, vo), main(GLA_DV, ro),
                  pl.BlockSpec((seq, A_COLS), lambda b, h: (b, 0)),
                  meta(GLA_DK, qo), meta(GLA_DK, ko), meta(GLA_DV, vo), meta(GLA_DV, ro),
                  pl.BlockSpec((N_META, A_COLS), lambda b, h: (mrow(b), 0)),
                  pl.BlockSpec((None, A_COLS, 2 * GLA_DK), lambda b, h: (h, 0, 0)),
                  pl.BlockSpec((None, 1, 2 * GLA_DK), lambda b, h: (h, 0, 0)),
                  pl.BlockSpec((1, GLA_DV), lambda b, h: (0, 0)),
                  pl.BlockSpec((CHUNK // 2, CHUNK // 2), lambda b, h: (0, 0))],
        out_specs=[pl.BlockSpec((seq, GLA_DV), lambda b, h: (b, h)),
                   pl.BlockSpec((N_META, GLA_DV), lambda b, h: (b, h))],
        out_shape=[jax.ShapeDtypeStruct((bsz * seq, GLA_WIDTH), BF16),
                   jax.ShapeDtypeStruct((bsz * N_META, GLA_WIDTH), BF16)],
        scratch_shapes=[pltpu.VMEM((n, CHUNK, GLA_DV), F32),
                        pltpu.VMEM((n, CHUNK, 2 * GLA_DK), BF16),
                        pltpu.VMEM((n, GLA_DV, 2 * GLA_DK), F32),
                        pltpu.VMEM((n, 1, 2 * GLA_DK), F32)],
        compiler_params=_params("parallel", "parallel"),
        name="gla",
    )(p_main, p_main, p_main, p_main, a_main, p_meta, p_meta, p_meta, p_meta, a_meta,
      w2blk, b2blk, head_g, _level_matrix())


def _conv_rows(y_scr, w, start, rows):
    n = rows + 2 * HALO
    win = y_scr[pl.ds(start - HALO, n), :]
    prev = pltpu.roll(win, 1, 0)[HALO:HALO + rows]
    nxt = pltpu.roll(win, n - 1, 0)[HALO:HALO + rows]
    return prev * w[0:1] + win[HALO:HALO + rows] * w[1:2] + nxt * w[2:3]


def _frame(y_scr, y_meta, seq):
    zeros = jnp.zeros((HALO, y_scr.shape[1]), F32)
    y_scr[0:HALO, :] = zeros
    y_scr[HALO:2 * HALO, :] = y_meta
    y_scr[2 * HALO + seq:3 * HALO + seq, :] = zeros


def _mixconv_body(cb_ref, cc_ref, ch_ref, cbm_ref, ccm_ref, chm_ref, w_ref, o_ref, om_ref, y_scr,
                  *, rows):
    seq = cb_ref.shape[0]
    w = w_ref[...]
    _frame(y_scr, ccm_ref[...].astype(F32) * chm_ref[...].astype(F32), seq)

    def fill(i, carry):
        sl = pl.ds(pl.multiple_of(i * rows, rows), rows)
        dst = pl.ds(pl.multiple_of(2 * HALO + i * rows, HALO), rows)
        y_scr[dst, :] = cc_ref[sl, :].astype(F32) * ch_ref[sl, :].astype(F32)
        return carry
    lax.fori_loop(0, seq // rows, fill, 0)

    om_ref[...] = (cbm_ref[...].astype(F32) * _conv_rows(y_scr, w, HALO, N_META)).astype(BF16)

    def tile(i, carry):
        sl = pl.ds(pl.multiple_of(i * rows, rows), rows)
        start = pl.multiple_of(2 * HALO + i * rows, HALO)
        o_ref[sl, :] = (cb_ref[sl, :].astype(F32) * _conv_rows(y_scr, w, start, rows)).astype(BF16)
        return carry
    lax.fori_loop(0, seq // rows, tile, 0)


def _mixconv(p_main, p_meta, conv_w, *, bsz, seq, shared_meta, rows=256):
    cbo, cco, cho = CB_OFF // CONV_COLS, CC_OFF // CONV_COLS, CH_OFF // CONV_COLS
    mrow = (lambda b: 0) if shared_meta else (lambda b: b)
    main = lambda off: pl.BlockSpec((seq, CONV_COLS), lambda b, j: (b, off + j))
    meta = lambda off: pl.BlockSpec((N_META, CONV_COLS), lambda b, j: (mrow(b), off + j))
    return pl.pallas_call(
        functools.partial(_mixconv_body, rows=rows),
        grid=(bsz, CONV_WIDTH // CONV_COLS),
        in_specs=[main(cbo), main(cco), main(cho), meta(cbo), meta(cco), meta(cho),
                  pl.BlockSpec((3, CONV_COLS), lambda b, j: (0, j))],
        out_specs=[pl.BlockSpec((seq, CONV_COLS), lambda b, j: (b, j)),
                   pl.BlockSpec((N_META, CONV_COLS), lambda b, j: (b, j))],
        out_shape=[jax.ShapeDtypeStruct((bsz * seq, CONV_WIDTH), BF16),
                   jax.ShapeDtypeStruct((bsz * N_META, CONV_WIDTH), BF16)],
        scratch_shapes=[pltpu.VMEM((seq + 3 * HALO, CONV_COLS), F32)],
        compiler_params=_params("parallel", "parallel"),
        name="mixconv",
    )(p_main, p_main, p_main, p_meta, p_meta, p_meta, conv_w)


def _norm_block_rows(src_ref, dst_ref, g_ref, rows):
    def body(i, carry):
        sl = pl.ds(pl.multiple_of(i * rows, rows), rows)
        dst_ref[sl, :] = _rmsnorm_rows(src_ref[sl, :], g_ref[...]).astype(dst_ref.dtype)
        return carry
    lax.fori_loop(0, src_ref.shape[0] // rows, body, 0)


def _out_proj_body(o_ref, c_ref, wo_ref, wc_ref, x_ref, g_ref, x1_ref, h2_ref, *, tn, rows):
    j = pl.program_id(1)
    acc = jnp.dot(o_ref[...], wo_ref[...], preferred_element_type=F32)
    acc = acc + jnp.dot(c_ref[...], wc_ref[...], preferred_element_type=F32)
    x1_ref[:, pl.ds(pl.multiple_of(j * tn, tn), tn)] = x_ref[...] + acc

    @pl.when(j == pl.num_programs(1) - 1)
    def _():
        _norm_block_rows(x1_ref, h2_ref, g_ref, rows)


def _out_proj(o, conv, w_out, x, g, *, tm):
    m = x.shape[0]
    tn = OUT_PROJ_TN
    return pl.pallas_call(
        functools.partial(_out_proj_body, tn=tn, rows=min(NORM_ROWS, tm)),
        grid=(m // tm, D_MODEL // tn),
        in_specs=[pl.BlockSpec((tm, GLA_WIDTH), lambda i, j: (i, 0)),
                  pl.BlockSpec((tm, CONV_WIDTH), lambda i, j: (i, 0)),
                  pl.BlockSpec((GLA_WIDTH, tn), lambda i, j: (0, j)),
                  pl.BlockSpec((CONV_WIDTH, tn), lambda i, j: (1, j)),
                  pl.BlockSpec((tm, tn), lambda i, j: (i, j)),
                  pl.BlockSpec((1, D_MODEL), lambda i, j: (0, 0))],
        out_specs=[pl.BlockSpec((tm, D_MODEL), lambda i, j: (i, 0)),
                   pl.BlockSpec((tm, D_MODEL), lambda i, j: (i, 0))],
        out_shape=[jax.ShapeDtypeStruct((m, D_MODEL), F32),
                   jax.ShapeDtypeStruct((m, D_MODEL), BF16)],
        compiler_params=_params("parallel", "arbitrary"),
        name="out_proj",
    )(o, conv, w_out, w_out, x, g)


def _up_proj_body(h_ref, hp_ref, hn_ref, hm_ref, wu_ref, wg_ref, cw_ref, cb_ref, act_ref, lhs_scr,
                  *, tiles_per_seq, rows):
    tm = h_ref.shape[0]
    i = pl.program_id(0)

    @pl.when(pl.program_id(1) == 0)
    def _():
        first = (i % tiles_per_seq) == 0
        last = (i % tiles_per_seq) == tiles_per_seq - 1

        @pl.when(first)
        def _():
            lhs_scr[0:HALO, :] = hm_ref[...]

        @pl.when(jnp.logical_not(first))
        def _():
            lhs_scr[0:HALO, :] = hp_ref[...]

        @pl.when(last)
        def _():
            lhs_scr[HALO + tm:2 * HALO + tm, :] = jnp.zeros((HALO, D_MODEL), BF16)

        @pl.when(jnp.logical_not(last))
        def _():
            lhs_scr[HALO + tm:2 * HALO + tm, :] = hn_ref[...]

        def copy_rows(r, carry):
            sl = pl.ds(pl.multiple_of(r * rows, rows), rows)
            lhs_scr[pl.ds(pl.multiple_of(HALO + r * rows, HALO), rows), :] = h_ref[sl, :]
            return carry
        lax.fori_loop(0, tm // rows, copy_rows, 0)

    def column_blocks(count):
        n = tm + 2 * HALO
        order = None
        w_shift = (UP_BLOCKS_PER_STEP - count) * UP_PROJ_TN
        for b in range(count):
            cols = slice(b * UP_PROJ_TN, (b + 1) * UP_PROJ_TN)
            w_cols = slice(w_shift + b * UP_PROJ_TN, w_shift + (b + 1) * UP_PROJ_TN)
            g = jnp.dot(lhs_scr[...], wg_ref[:, w_cols], preferred_element_type=F32)
            u = jnp.dot(h_ref[...], wu_ref[:, w_cols], preferred_element_type=F32)
            if order is not None:
                g = g + order
            w = cw_ref[:, cols]
            conv = (pltpu.roll(g, 1, 0)[HALO:HALO + tm] * w[0:1] + g[HALO:HALO + tm] * w[1:2]
                    + pltpu.roll(g, n - 1, 0)[HALO:HALO + tm] * w[2:3])
            act = _silu(conv + cb_ref[:, cols]) * u
            act_ref[:, cols] = act.astype(BF16)
            if b + 1 < count:
                acc = act[0:F32_TILE_ROWS]
                for r in range(1, tm // F32_TILE_ROWS):
                    acc = jnp.maximum(acc, act[r * F32_TILE_ROWS:(r + 1) * F32_TILE_ROWS])
                bits = pltpu.bitcast(acc, jnp.int32)
                gone = lax.shift_right_logical(lax.shift_right_logical(bits, 16), 16)
                order = jnp.max(gone.astype(F32), axis=0, keepdims=True)

    full_steps, left = divmod(D_FF // UP_PROJ_TN, UP_BLOCKS_PER_STEP)
    j = pl.program_id(1)

    @pl.when(j < full_steps)
    def _():
        column_blocks(UP_BLOCKS_PER_STEP)

    if left:
        @pl.when(j == full_steps)
        def _():
            column_blocks(left)


def _up_proj(h_main, h_meta, w_up, conv_w, conv_b, *, seq, meta_off, tm):
    m = h_main.shape[0]
    tn = UP_PROJ_TN * UP_BLOCKS_PER_STEP
    nblk = pl.cdiv(D_FF, tn)
    tiles_per_seq = seq // tm
    halo_per_tile = tm // HALO
    n_halo = m // HALO
    blocks_per_half = D_FF // UP_PROJ_TN

    def w_window(half):
        def index(i, j):
            first = jnp.minimum(j * UP_BLOCKS_PER_STEP, blocks_per_half - UP_BLOCKS_PER_STEP)
            return 0, UP_PROJ_TN * (half * blocks_per_half + first)
        return pl.BlockSpec((pl.Element(D_MODEL), pl.Element(tn)), index)

    return pl.pallas_call(
        functools.partial(_up_proj_body, tiles_per_seq=tiles_per_seq, rows=64),
        grid=(m // tm, nblk),
        in_specs=[pl.BlockSpec((tm, D_MODEL), lambda i, j: (i, 0)),
                  pl.BlockSpec((HALO, D_MODEL),
                               lambda i, j: (jnp.maximum(i * halo_per_tile - 1, 0), 0)),
                  pl.BlockSpec((HALO, D_MODEL),
                               lambda i, j: (jnp.minimum((i + 1) * halo_per_tile, n_halo - 1), 0)),
                  pl.BlockSpec((HALO, D_MODEL), lambda i, j: (meta_off + i // tiles_per_seq, 0)),
                  w_window(0),
                  w_window(1),
                  pl.BlockSpec((3, tn), lambda i, j: (0, j)),
                  pl.BlockSpec((1, tn), lambda i, j: (0, j))],
        out_specs=pl.BlockSpec((tm, tn), lambda i, j: (i, j)),
        out_shape=jax.ShapeDtypeStruct((m, D_FF), BF16),
        scratch_shapes=[pltpu.VMEM((tm + 2 * HALO, D_MODEL), BF16)],
        compiler_params=_params("parallel", "arbitrary"),
        name="up_proj",
    )(h_main, h_main, h_main, h_meta, w_up, w_up, conv_w, conv_b)


def _down_proj_body(act_ref, w_ref, x1_ref, g_ref, y_ref, *, tn, rows):
    j = pl.program_id(1)
    y_ref[:, pl.ds(pl.multiple_of(j * tn, tn), tn)] = (
        x1_ref[...] + jnp.dot(act_ref[...], w_ref[...], preferred_element_type=F32))

    @pl.when(j == pl.num_programs(1) - 1)
    def _():
        _norm_block_rows(y_ref, y_ref, g_ref, rows)


def _down_proj(act, w_down, x1, g, *, tm):
    m = act.shape[0]
    tn = DOWN_PROJ_TN
    return pl.pallas_call(
        functools.partial(_down_proj_body, tn=tn, rows=NORM_ROWS),
        grid=(m // tm, D_MODEL // tn),
        in_specs=[pl.BlockSpec((tm, D_FF), lambda i, j: (i, 0)),
                  pl.BlockSpec((D_FF, tn), lambda i, j: (0, j)),
                  pl.BlockSpec((tm, tn), lambda i, j: (i, j)),
                  pl.BlockSpec((1, D_MODEL), lambda i, j: (0, 0))],
        out_specs=pl.BlockSpec((tm, D_MODEL), lambda i, j: (i, 0)),
        out_shape=jax.ShapeDtypeStruct((m, D_MODEL), F32),
        compiler_params=_params("parallel", "arbitrary"),
        name="down_proj",
    )(act, w_down, x1, g)


def _cast_in_weights_body(wt_ref, wat_ref, wm_ref, wa_ref):
    wm_ref[...] = wt_ref[...].T.astype(BF16)

    @pl.when(pl.program_id(0) == 0)
    def _():
        wa_ref[...] = wat_ref[...].astype(BF16)


def _cast_in_weights(w_in_t, *, tn=512):
    a_lo = V_OFF + 2 * GLA_WIDTH
    src_row = lambda c: A_COLS * (c * (tn // A_COLS) + jnp.where(c * tn >= a_lo, 1, 0))
    return pl.pallas_call(
        _cast_in_weights_body,
        grid=(P_COLS // tn,),
        in_specs=[pl.BlockSpec((pl.Element(tn), pl.Element(D_MODEL)), lambda c: (src_row(c), 0)),
                  pl.BlockSpec((A_COLS, D_MODEL), lambda c: (a_lo // A_COLS, 0))],
        out_specs=[pl.BlockSpec((D_MODEL, tn), lambda c: (0, c)),
                   pl.BlockSpec((A_COLS, D_MODEL), lambda c: (0, 0))],
        out_shape=[jax.ShapeDtypeStruct((D_MODEL, P_COLS), BF16),
                   jax.ShapeDtypeStruct((A_COLS, D_MODEL), BF16)],
        compiler_params=_params("arbitrary"),
        name="cast_in_weights",
    )(w_in_t, w_in_t)


def _prepare_weights(mix_norm_g, w_in, w_gate2, b_gate2, head_norm_g, conv_mix_w, w_out,
                     ffn_norm_g, w_up, ffn_conv_w, ffn_conv_b, w_down, final_norm_g):
    w_main, w_a = _cast_in_weights(w_in.T)
    wf = w_gate2[0].reshape(GATE_RANK, GLA_HEADS, GLA_DK).transpose(1, 0, 2)
    wb = w_gate2[1].reshape(GATE_RANK, GLA_HEADS, GLA_DK).transpose(1, 0, 2)
    zero = jnp.zeros_like(wf)
    w2blk = jnp.concatenate([jnp.concatenate([wf, zero], axis=2),
                             jnp.concatenate([zero, wb], axis=2)], axis=1).astype(BF16)
    b2blk = jnp.concatenate([b_gate2[0].reshape(GLA_HEADS, 1, GLA_DK),
                             b_gate2[1].reshape(GLA_HEADS, 1, GLA_DK)], axis=2).astype(F32)
    return dict(
        mix_g=mix_norm_g.reshape(1, D_MODEL), w_main=w_main, w_a=w_a, w2blk=w2blk, b2blk=b2blk,
        head_g=head_norm_g.reshape(1, GLA_DV), conv_mix_w=conv_mix_w,
        w_out=w_out.astype(BF16),
        ffn_g=ffn_norm_g.reshape(1, D_MODEL), w_up=w_up.astype(BF16),
        ffn_conv_w=ffn_conv_w,
        ffn_conv_b=ffn_conv_b.reshape(1, D_FF), w_down=w_down.astype(BF16),
        final_g=final_norm_g.reshape(1, D_MODEL))


def kernel(x_prompt, x_sample, meta_tokens, mix_norm_g, w_in, w_gate2, b_gate2, head_norm_g,
           conv_mix_w, w_out, ffn_norm_g, w_up, ffn_conv_w, ffn_conv_b, w_down, final_norm_g):
    w = _prepare_weights(mix_norm_g[0], w_in[0], w_gate2[0], b_gate2[0], head_norm_g[0],
                         conv_mix_w[0], w_out[0], ffn_norm_g[0], w_up[0], ffn_conv_w[0],
                         ffn_conv_b[0], w_down[0], final_norm_g)
    trunks = (x_prompt, x_sample)
    p_meta16, a_meta16 = _in_proj(meta_tokens, w["mix_g"], w["w_main"], w["w_a"], tm=N_META)

    mixed = []
    for x in trunks:
        bsz, seq, _ = x.shape
        x_main = x.reshape(bsz * seq, D_MODEL)
        p_main, a_main = _in_proj(x_main, w["mix_g"], w["w_main"], w["w_a"], tm=1024)
        o_main, o_meta = _gla(p_main, a_main, p_meta16, a_meta16, w["w2blk"], w["b2blk"],
                              w["head_g"], bsz=bsz, seq=seq, shared_meta=True)
        c_main, c_meta = _mixconv(p_main, p_meta16, w["conv_mix_w"], bsz=bsz, seq=seq,
                                  shared_meta=True)
        mixed.append((x_main, o_main, c_main, o_meta, c_meta))

    n_seq = sum(x.shape[0] for x in trunks)
    _, h2_meta = _out_proj(jnp.concatenate([t[3] for t in mixed], axis=0),
                           jnp.concatenate([t[4] for t in mixed], axis=0),
                           w["w_out"], jnp.tile(meta_tokens, (n_seq, 1)), w["ffn_g"],
                           tm=n_seq * N_META)

    outs = []
    meta_off = 0
    for x, (x_main, o_main, c_main, _, _) in zip(trunks, mixed):
        bsz, seq, _ = x.shape
        x1, h2 = _out_proj(o_main, c_main, w["w_out"], x_main, w["ffn_g"], tm=512)
        act = _up_proj(h2, h2_meta, w["w_up"], w["ffn_conv_w"], w["ffn_conv_b"],
                       seq=seq, meta_off=meta_off, tm=1024)
        y = _down_proj(act, w["w_down"], x1, w["final_g"], tm=512)
        outs.append(y.reshape(bsz, seq, D_MODEL))
        meta_off += bsz
    return tuple(outs)
```

```python
import functools

import numpy as np
import jax
import jax.numpy as jnp
from jax import lax
from jax.experimental import pallas as pl
from jax.experimental.pallas import tpu as pltpu

F32 = jnp.float32
BF16 = jnp.bfloat16

D_MODEL = 4096
N_META = 16
GLA_HEADS = 8
GLA_DK = 128
GLA_DV = 256
GLA_KEY_WIDTH = GLA_HEADS * GLA_DK
GLA_WIDTH = GLA_HEADS * GLA_DV
CONV_WIDTH = 2048
GATE_RANK = 16
GATE_TAU = 16.0
D_FF = 11008
EPS = 1e-6

P_COLS = 2 * GLA_KEY_WIDTH + 2 * GLA_WIDTH + 3 * CONV_WIDTH
A_COLS = 2 * GATE_RANK
Q_OFF, K_OFF, V_OFF, R_OFF = 0, GLA_KEY_WIDTH, 2 * GLA_KEY_WIDTH, 2 * GLA_KEY_WIDTH + GLA_WIDTH
CB_OFF = R_OFF + GLA_WIDTH
CC_OFF = CB_OFF + CONV_WIDTH
CH_OFF = CC_OFF + CONV_WIDTH

CHUNK = 256
LEVELS = 8
F32_TILE_ROWS = 8
BF16_TILE_ROWS = 16
TILE_LEVELS = 3
LOG2_E = 1.4426950408889634
HALO = 16
CONV_COLS = 256
NORM_ROWS = 128
V7X_VMEM_LIMIT = 56 * 1024 * 1024
IN_PROJ_TN = 1024
OUT_PROJ_TN = 1024
UP_PROJ_TN = 256
UP_BLOCKS_PER_STEP = 3
DOWN_PROJ_TN = 512


def _params(*sem):
    return pltpu.CompilerParams(dimension_semantics=sem, vmem_limit_bytes=V7X_VMEM_LIMIT)


def _rmsnorm_rows(x, g):
    ms = jnp.mean(x * x, axis=-1, keepdims=True)
    return x * lax.rsqrt(ms + EPS) * g


def _silu(x):
    return x * (1.0 / (1.0 + jnp.exp(-x)))


def _nt_dot(a, b):
    return lax.dot_general(a, b, (((1,), (1,)), ((), ())), preferred_element_type=F32)


def _in_proj_body(x_ref, g_ref, w_ref, wa_ref, p_ref, a_ref, h_scr, *, rows):
    @pl.when(pl.program_id(1) == 0)
    def _():
        def norm_rows(i, carry):
            sl = pl.ds(pl.multiple_of(i * rows, rows), rows)
            h_scr[sl, :] = _rmsnorm_rows(x_ref[sl, :], g_ref[...]).astype(BF16)
            return carry
        lax.fori_loop(0, x_ref.shape[0] // rows, norm_rows, 0)
        a_ref[...] = _nt_dot(h_scr[...], wa_ref[...])

    p_ref[...] = jnp.dot(h_scr[...], w_ref[...], preferred_element_type=F32).astype(BF16)


def _in_proj(x, g, w_main, w_a, *, tm):
    m = x.shape[0]
    tn = IN_PROJ_TN
    return pl.pallas_call(
        functools.partial(_in_proj_body, rows=min(NORM_ROWS, tm)),
        grid=(m // tm, P_COLS // tn),
        in_specs=[pl.BlockSpec((tm, D_MODEL), lambda i, j: (i, 0), pipeline_mode=pl.Buffered(1)),
                  pl.BlockSpec((1, D_MODEL), lambda i, j: (0, 0)),
                  pl.BlockSpec((D_MODEL, tn), lambda i, j: (0, j)),
                  pl.BlockSpec((A_COLS, D_MODEL), lambda i, j: (0, 0))],
        out_specs=[pl.BlockSpec((tm, tn), lambda i, j: (i, j)),
                   pl.BlockSpec((tm, A_COLS), lambda i, j: (i, 0))],
        out_shape=[jax.ShapeDtypeStruct((m, P_COLS), BF16),
                   jax.ShapeDtypeStruct((m, A_COLS), F32)],
        scratch_shapes=[pltpu.VMEM((tm, D_MODEL), BF16)],
        compiler_params=_params("parallel", "arbitrary"),
        name="in_proj",
    )(x, g, w_main, w_a)


def _log2_decay(a, w2, b2):
    z = jnp.dot(a.astype(BF16), w2, preferred_element_type=F32) + b2
    return (jnp.minimum(z, 0.0) - jnp.log(1.0 + jnp.exp(-jnp.abs(z)))) * (LOG2_E / GATE_TAU)


def _chunk_intra(q, k, v, lg, lvd):
    half = CHUNK // 2
    shape = (CHUNK, 2 * GLA_DK)
    row = lax.broadcasted_iota(jnp.int32, shape, 0)
    isb = lax.broadcasted_iota(jnp.int32, shape, 1) >= GLA_DK
    qb = q.astype(BF16)
    q2b = jnp.concatenate([qb, qb], axis=1)
    kb = k.astype(BF16)
    k2b = jnp.concatenate([kb, kb], axis=1)
    zero = jnp.zeros(shape, BF16)
    diag = [jnp.zeros((half, half), F32), jnp.zeros((half, half), F32)]

    def keep_level(lvl, ql, kl):
        for d in range(2):
            rows = slice(d * half, (d + 1) * half)
            diag[d] = jnp.where(lvd == lvl, _nt_dot(ql[rows], kl[rows]), diag[d])

    def masked_level(lvl, e):
        upper = (row & (1 << lvl)) != 0
        qside = jnp.logical_xor(upper, isb)
        keep_level(lvl, jnp.where(qside, q2b * e, zero), k2b * e)
        return upper

    n_tiles = CHUNK // F32_TILE_ROWS
    tiled = (n_tiles, F32_TILE_ROWS, 2 * GLA_DK)
    row_t = lax.broadcasted_iota(jnp.int32, (1,) + tiled[1:], 1)
    isb_t = lax.broadcasted_iota(jnp.int32, (1,) + tiled[1:], 2) >= GLA_DK
    tb = lg.reshape(tiled)
    p = jnp.where(isb_t, 0.0, tb)
    for lvl in range(TILE_LEVELS):
        m = 1 << lvl
        upper_t = (row_t & m) != 0
        e = jnp.exp2(jnp.where(upper_t, p, tb - p)).reshape(shape).astype(BF16)
        masked_level(lvl, e)
        tb_prev = pltpu.roll(tb, m, 1)
        tb_next = pltpu.roll(tb, F32_TILE_ROWS - m, 1)
        p = p + jnp.where(upper_t, tb_prev, 0.0)
        tb = tb + jnp.where(upper_t, tb_prev, tb_next)

    off = [jnp.zeros(tiled[1:], F32)]
    for j in range(n_tiles - 1):
        off.append(off[-1] + tb[j])
    total = off[-1] + tb[n_tiles - 1]
    pre_t = [p[j] + off[j] for j in range(n_tiles)]

    for lvl in range(TILE_LEVELS, LEVELS):
        m = 1 << lvl
        tpb = m // F32_TILE_ROWS
        args = []
        for j in range(n_tiles):
            later_start = off[(j // (2 * tpb) * 2 + 1) * tpb]
            args.append(pre_t[j] - later_start if (j // tpb) % 2 else later_start - pre_t[j])
        e = jnp.exp2(jnp.concatenate(args, axis=0)).astype(BF16)
        if m < BF16_TILE_ROWS:
            masked_level(lvl, e)
        elif m < half:
            zeros_m = jnp.zeros((m, GLA_DK), BF16)
            qf, qr = [], []
            for g in range(CHUNK // (2 * m)):
                lo = slice(2 * m * g, 2 * m * g + m)
                up = slice(2 * m * g + m, 2 * m * (g + 1))
                qf += [zeros_m, qb[up] * e[up, :GLA_DK]]
                qr += [qb[lo] * e[lo, GLA_DK:], zeros_m]
            ql = jnp.concatenate([jnp.concatenate(qf, axis=0), jnp.concatenate(qr, axis=0)], axis=1)
            keep_level(lvl, ql, k2b * e)
        else:
            s_lower = _nt_dot(qb[half:] * e[half:, :GLA_DK], kb[:half] * e[:half, :GLA_DK])
            s_upper = _nt_dot(qb[:half] * e[:half, GLA_DK:], kb[half:] * e[half:, GLA_DK:])
    scores = jnp.concatenate([jnp.concatenate([diag[0], s_upper], axis=1),
                              jnp.concatenate([s_lower, diag[1]], axis=1)], axis=0)

    vb = v.astype(BF16)
    qk = jnp.sum(q * k, axis=-1, keepdims=True)
    o_part = jnp.dot(scores.astype(BF16), vb, preferred_element_type=F32) + qk * v
    pre = jnp.concatenate(pre_t, axis=0)
    rest = jnp.concatenate([total] * n_tiles, axis=0) - pre
    khat = k2b * jnp.exp2(jnp.concatenate([rest[:, :GLA_DK], pre[:, GLA_DK:]], axis=1)).astype(BF16)
    qhat = q2b * jnp.exp2(jnp.concatenate([pre[:, :GLA_DK], rest[:, GLA_DK:]], axis=1)).astype(BF16)
    dstate = lax.dot_general(vb, khat, (((0,), (0,)), ((), ())), preferred_element_type=F32)
    return o_part, qhat, dstate, jnp.exp2(total[0:1, :])


def _gla_body(q_ref, k_ref, v_ref, r_ref, a_ref, qm_ref, km_ref, vm_ref, rm_ref, am_ref,
              w2_ref, b2_ref, hg_ref, lv_ref, o_ref, om_ref,
              opart_scr, qhat_scr, st_scr, dec_scr):
    n_main = q_ref.shape[0] // CHUNK
    n = n_main + 1
    scale = GLA_DK ** -0.5
    w2 = w2_ref[...]
    b2 = b2_ref[...]
    pad = CHUNK - N_META

    def intra(c, q, k, v, lg):
        o_part, qhat, dstate, dec = _chunk_intra(q * scale, k, v, lg, lv_ref[...])
        opart_scr[c] = o_part
        qhat_scr[c] = qhat
        st_scr[c] = dstate
        dec_scr[c] = dec

    def behind_zeros(x):
        return jnp.concatenate([jnp.zeros((pad, x.shape[1]), F32), x.astype(F32)], axis=0)

    intra(0, behind_zeros(qm_ref[...]), behind_zeros(km_ref[...]), behind_zeros(vm_ref[...]),
          behind_zeros(_log2_decay(am_ref[...], w2, b2)))

    def intra_main(c, carry):
        sl = pl.ds(pl.multiple_of(c * CHUNK, CHUNK), CHUNK)
        intra(c + 1, q_ref[sl, :].astype(F32), k_ref[sl, :].astype(F32), v_ref[sl, :].astype(F32),
              _log2_decay(a_ref[sl, :], w2, b2))
        return carry
    lax.fori_loop(0, n_main, intra_main, 0)

    def scan_fwd(c, s):
        d = st_scr[c, :, 0:GLA_DK]
        st_scr[c, :, 0:GLA_DK] = s
        return dec_scr[c][:, 0:GLA_DK] * s + d
    lax.fori_loop(0, n, scan_fwd, jnp.zeros((GLA_DV, GLA_DK), F32))

    def scan_bwd(i, s):
        c = n - 1 - i
        d = st_scr[c, :, GLA_DK:]
        st_scr[c, :, GLA_DK:] = s
        return dec_scr[c][:, GLA_DK:] * s + d
    lax.fori_loop(0, n, scan_bwd, jnp.zeros((GLA_DV, GLA_DK), F32))

    def normed(c):
        o = opart_scr[c] + lax.dot_general(qhat_scr[c], st_scr[c].astype(BF16),
                                           (((1,), (1,)), ((), ())), preferred_element_type=F32)
        return _rmsnorm_rows(o, hg_ref[...])

    om_ref[...] = (normed(0)[pad:, :] * _silu(rm_ref[...].astype(F32))).astype(BF16)

    def finish_main(c, carry):
        sl = pl.ds(pl.multiple_of(c * CHUNK, CHUNK), CHUNK)
        o_ref[sl, :] = (normed(c + 1) * _silu(r_ref[sl, :].astype(F32))).astype(BF16)
        return carry
    lax.fori_loop(0, n_main, finish_main, 0)


def _level_matrix():
    half = CHUNK // 2
    t = np.arange(half)
    x = t[:, None] ^ t[None, :]
    lv = np.full((half, half), -1, np.int32)
    nz = x > 0
    lv[nz] = np.floor(np.log2(x[nz])).astype(np.int32)
    return jnp.asarray(lv)


def _gla(p_main, a_main, p_meta, a_meta, w2blk, b2blk, head_g, *, bsz, seq, shared_meta):
    n = seq // CHUNK + 1
    qo, ko = Q_OFF // GLA_DK, K_OFF // GLA_DK
    vo, ro = V_OFF // GLA_DV, R_OFF // GLA_DV
    mrow = (lambda b: 0) if shared_meta else (lambda b: b)
    main = lambda width, off: pl.BlockSpec((seq, width), lambda b, h: (b, off + h))
    meta = lambda width, off: pl.BlockSpec((N_META, width), lambda b, h: (mrow(b), off + h))
    return pl.pallas_call(
        _gla_body,
        grid=(bsz, GLA_HEADS),
        in_specs=[main(GLA_DK, qo), main(GLA_DK, ko), main(GLA_DV, vo), main(GLA_DV, ro),
                  pl.BlockSpec((seq, A_COLS), lambda b, h: (b, 0)),
                  meta(GLA_DK, qo), meta(GLA_DK, ko), meta(GLA_DV, vo), meta(GLA_DV, ro),
                  pl.BlockSpec((N_META, A_COLS), lambda b, h: (mrow(b), 0)),
                  pl.BlockSpec((None, A_COLS, 2 * GLA_DK), lambda b, h: (h, 0, 0)),
                  pl.BlockSpec((None, 1, 2 * GLA_DK), lambda b, h: (h, 0, 0)),
                  pl.BlockSpec((1, GLA_DV), lambda b, h: (0, 0)),
                  pl.BlockSpec((CHUNK // 2, CHUNK // 2), lambda b, h: (0, 0))],
        out_specs=[pl.BlockSpec((seq, GLA_DV), lambda b, h: (b, h)),
                   pl.BlockSpec((N_META, GLA_DV), lambda b, h: (b, h))],
        out_shape=[jax.ShapeDtypeStruct((bsz * seq, GLA_WIDTH), BF16),
                   jax.ShapeDtypeStruct((bsz * N_META, GLA_WIDTH), BF16)],
        scratch_shapes=[pltpu.VMEM((n, CHUNK, GLA_DV), F32),
                        pltpu.VMEM((n, CHUNK, 2 * GLA_DK), BF16),
                        pltpu.VMEM((n, GLA_DV, 2 * GLA_DK), F32),
                        pltpu.VMEM((n, 1, 2 * GLA_DK), F32)],
        compiler_params=_params("parallel", "parallel"),
        name="gla",
    )(p_main, p_main, p_main, p_main, a_main, p_meta, p_meta, p_meta, p_meta, a_meta,
      w2blk, b2blk, head_g, _level_matrix())


def _conv_rows(y_scr, w, start, rows):
    n = rows + 2 * HALO
    win = y_scr[pl.ds(start - HALO, n), :]
    prev = pltpu.roll(win, 1, 0)[HALO:HALO + rows]
    nxt = pltpu.roll(win, n - 1, 0)[HALO:HALO + rows]
    return prev * w[0:1] + win[HALO:HALO + rows] * w[1:2] + nxt * w[2:3]


def _frame(y_scr, y_meta, seq):
    zeros = jnp.zeros((HALO, y_scr.shape[1]), F32)
    y_scr[0:HALO, :] = zeros
    y_scr[HALO:2 * HALO, :] = y_meta
    y_scr[2 * HALO + seq:3 * HALO + seq, :] = zeros


def _mixconv_body(cb_ref, cc_ref, ch_ref, cbm_ref, ccm_ref, chm_ref, w_ref, o_ref, om_ref, y_scr,
                  *, rows):
    seq = cb_ref.shape[0]
    w = w_ref[...]
    _frame(y_scr, ccm_ref[...].astype(F32) * chm_ref[...].astype(F32), seq)

    def fill(i, carry):
        sl = pl.ds(pl.multiple_of(i * rows, rows), rows)
        dst = pl.ds(pl.multiple_of(2 * HALO + i * rows, HALO), rows)
        y_scr[dst, :] = cc_ref[sl, :].astype(F32) * ch_ref[sl, :].astype(F32)
        return carry
    lax.fori_loop(0, seq // rows, fill, 0)

    om_ref[...] = (cbm_ref[...].astype(F32) * _conv_rows(y_scr, w, HALO, N_META)).astype(BF16)

    def tile(i, carry):
        sl = pl.ds(pl.multiple_of(i * rows, rows), rows)
        start = pl.multiple_of(2 * HALO + i * rows, HALO)
        o_ref[sl, :] = (cb_ref[sl, :].astype(F32) * _conv_rows(y_scr, w, start, rows)).astype(BF16)
        return carry
    lax.fori_loop(0, seq // rows, tile, 0)


def _mixconv(p_main, p_meta, conv_w, *, bsz, seq, shared_meta, rows=256):
    cbo, cco, cho = CB_OFF // CONV_COLS, CC_OFF // CONV_COLS, CH_OFF // CONV_COLS
    mrow = (lambda b: 0) if shared_meta else (lambda b: b)
    main = lambda off: pl.BlockSpec((seq, CONV_COLS), lambda b, j: (b, off + j))
    meta = lambda off: pl.BlockSpec((N_META, CONV_COLS), lambda b, j: (mrow(b), off + j))
    return pl.pallas_call(
        functools.partial(_mixconv_body, rows=rows),
        grid=(bsz, CONV_WIDTH // CONV_COLS),
        in_specs=[main(cbo), main(cco), main(cho), meta(cbo), meta(cco), meta(cho),
                  pl.BlockSpec((3, CONV_COLS), lambda b, j: (0, j))],
        out_specs=[pl.BlockSpec((seq, CONV_COLS), lambda b, j: (b, j)),
                   pl.BlockSpec((N_META, CONV_COLS), lambda b, j: (b, j))],
        out_shape=[jax.ShapeDtypeStruct((bsz * seq, CONV_WIDTH), BF16),
                   jax.ShapeDtypeStruct((bsz * N_META, CONV_WIDTH), BF16)],
        scratch_shapes=[pltpu.VMEM((seq + 3 * HALO, CONV_COLS), F32)],
        compiler_params=_params("parallel", "parallel"),
        name="mixconv",
    )(p_main, p_main, p_main, p_meta, p_meta, p_meta, conv_w)


def _norm_block_rows(src_ref, dst_ref, g_ref, rows):
    def body(i, carry):
        sl = pl.ds(pl.multiple_of(i * rows, rows), rows)
        dst_ref[sl, :] = _rmsnorm_rows(src_ref[sl, :], g_ref[...]).astype(dst_ref.dtype)
        return carry
    lax.fori_loop(0, src_ref.shape[0] // rows, body, 0)


def _out_proj_body(o_ref, c_ref, wo_ref, wc_ref, x_ref, g_ref, x1_ref, h2_ref, *, tn, rows):
    j = pl.program_id(1)
    acc = jnp.dot(o_ref[...], wo_ref[...], preferred_element_type=F32)
    acc = acc + jnp.dot(c_ref[...], wc_ref[...], preferred_element_type=F32)
    x1_ref[:, pl.ds(pl.multiple_of(j * tn, tn), tn)] = x_ref[...] + acc

    @pl.when(j == pl.num_programs(1) - 1)
    def _():
        _norm_block_rows(x1_ref, h2_ref, g_ref, rows)


def _out_proj(o, conv, w_out, x, g, *, tm):
    m = x.shape[0]
    tn = OUT_PROJ_TN
    return pl.pallas_call(
        functools.partial(_out_proj_body, tn=tn, rows=min(NORM_ROWS, tm)),
        grid=(m // tm, D_MODEL // tn),
        in_specs=[pl.BlockSpec((tm, GLA_WIDTH), lambda i, j: (i, 0)),
                  pl.BlockSpec((tm, CONV_WIDTH), lambda i, j: (i, 0)),
                  pl.BlockSpec((GLA_WIDTH, tn), lambda i, j: (0, j)),
                  pl.BlockSpec((CONV_WIDTH, tn), lambda i, j: (1, j)),
                  pl.BlockSpec((tm, tn), lambda i, j: (i, j)),
                  pl.BlockSpec((1, D_MODEL), lambda i, j: (0, 0))],
        out_specs=[pl.BlockSpec((tm, D_MODEL), lambda i, j: (i, 0)),
                   pl.BlockSpec((tm, D_MODEL), lambda i, j: (i, 0))],
        out_shape=[jax.ShapeDtypeStruct((m, D_MODEL), F32),
                   jax.ShapeDtypeStruct((m, D_MODEL), BF16)],
        compiler_params=_params("parallel", "arbitrary"),
        name="out_proj",
    )(o, conv, w_out, w_out, x, g)


def _up_proj_body(h_ref, hp_ref, hn_ref, hm_ref, wu_ref, wg_ref, cw_ref, cb_ref, act_ref, lhs_scr,
                  *, tiles_per_seq, rows):
    tm = h_ref.shape[0]
    i = pl.program_id(0)

    @pl.when(pl.program_id(1) == 0)
    def _():
        first = (i % tiles_per_seq) == 0
        last = (i % tiles_per_seq) == tiles_per_seq - 1

        @pl.when(first)
        def _():
            lhs_scr[0:HALO, :] = hm_ref[...]

        @pl.when(jnp.logical_not(first))
        def _():
            lhs_scr[0:HALO, :] = hp_ref[...]

        @pl.when(last)
        def _():
            lhs_scr[HALO + tm:2 * HALO + tm, :] = jnp.zeros((HALO, D_MODEL), BF16)

        @pl.when(jnp.logical_not(last))
        def _():
            lhs_scr[HALO + tm:2 * HALO + tm, :] = hn_ref[...]

        def copy_rows(r, carry):
            sl = pl.ds(pl.multiple_of(r * rows, rows), rows)
            lhs_scr[pl.ds(pl.multiple_of(HALO + r * rows, HALO), rows), :] = h_ref[sl, :]
            return carry
        lax.fori_loop(0, tm // rows, copy_rows, 0)

    def column_blocks(count):
        n = tm + 2 * HALO
        order = None
        w_shift = (UP_BLOCKS_PER_STEP - count) * UP_PROJ_TN
        for b in range(count):
            cols = slice(b * UP_PROJ_TN, (b + 1) * UP_PROJ_TN)
            w_cols = slice(w_shift + b * UP_PROJ_TN, w_shift + (b + 1) * UP_PROJ_TN)
            g = jnp.dot(lhs_scr[...], wg_ref[:, w_cols], preferred_element_type=F32)
            u = jnp.dot(h_ref[...], wu_ref[:, w_cols], preferred_element_type=F32)
            if order is not None:
                g = g + order
            w = cw_ref[:, cols]
            conv = (pltpu.roll(g, 1, 0)[HALO:HALO + tm] * w[0:1] + g[HALO:HALO + tm] * w[1:2]
                    + pltpu.roll(g, n - 1, 0)[HALO:HALO + tm] * w[2:3])
            act = _silu(conv + cb_ref[:, cols]) * u
            act_ref[:, cols] = act.astype(BF16)
            if b + 1 < count:
                acc = act[0:F32_TILE_ROWS]
                for r in range(1, tm // F32_TILE_ROWS):
                    acc = jnp.maximum(acc, act[r * F32_TILE_ROWS:(r + 1) * F32_TILE_ROWS])
                bits = pltpu.bitcast(acc, jnp.int32)
                gone = lax.shift_right_logical(lax.shift_right_logical(bits, 16), 16)
                order = jnp.max(gone.astype(F32), axis=0, keepdims=True)

    full_steps, left = divmod(D_FF // UP_PROJ_TN, UP_BLOCKS_PER_STEP)
    j = pl.program_id(1)

    @pl.when(j < full_steps)
    def _():
        column_blocks(UP_BLOCKS_PER_STEP)

    if left:
        @pl.when(j == full_steps)
        def _():
            column_blocks(left)


def _up_proj(h_main, h_meta, w_up, conv_w, conv_b, *, seq, meta_off, tm):
    m = h_main.shape[0]
    tn = UP_PROJ_TN * UP_BLOCKS_PER_STEP
    nblk = pl.cdiv(D_FF, tn)
    tiles_per_seq = seq // tm
    halo_per_tile = tm // HALO
    n_halo = m // HALO
    blocks_per_half = D_FF // UP_PROJ_TN

    def w_window(half):
        def index(i, j):
            first = jnp.minimum(j * UP_BLOCKS_PER_STEP, blocks_per_half - UP_BLOCKS_PER_STEP)
            return 0, UP_PROJ_TN * (half * blocks_per_half + first)
        return pl.BlockSpec((pl.Element(D_MODEL), pl.Element(tn)), index)

    return pl.pallas_call(
        functools.partial(_up_proj_body, tiles_per_seq=tiles_per_seq, rows=64),
        grid=(m // tm, nblk),
        in_specs=[pl.BlockSpec((tm, D_MODEL), lambda i, j: (i, 0)),
                  pl.BlockSpec((HALO, D_MODEL),
                               lambda i, j: (jnp.maximum(i * halo_per_tile - 1, 0), 0)),
                  pl.BlockSpec((HALO, D_MODEL),
                               lambda i, j: (jnp.minimum((i + 1) * halo_per_tile, n_halo - 1), 0)),
                  pl.BlockSpec((HALO, D_MODEL), lambda i, j: (meta_off + i // tiles_per_seq, 0)),
                  w_window(0),
                  w_window(1),
                  pl.BlockSpec((3, tn), lambda i, j: (0, j)),
                  pl.BlockSpec((1, tn), lambda i, j: (0, j))],
        out_specs=pl.BlockSpec((tm, tn), lambda i, j: (i, j)),
        out_shape=jax.ShapeDtypeStruct((m, D_FF), BF16),
        scratch_shapes=[pltpu.VMEM((tm + 2 * HALO, D_MODEL), BF16)],
        compiler_params=_params("parallel", "arbitrary"),
        name="up_proj",
    )(h_main, h_main, h_main, h_meta, w_up, w_up, conv_w, conv_b)


def _down_proj_body(act_ref, w_ref, x1_ref, g_ref, y_ref, *, tn, rows):
    j = pl.program_id(1)
    y_ref[:, pl.ds(pl.multiple_of(j * tn, tn), tn)] = (
        x1_ref[...] + jnp.dot(act_ref[...], w_ref[...], preferred_element_type=F32))

    @pl.when(j == pl.num_programs(1) - 1)
    def _():
        _norm_block_rows(y_ref, y_ref, g_ref, rows)


def _down_proj(act, w_down, x1, g, *, tm):
    m = act.shape[0]
    tn = DOWN_PROJ_TN
    return pl.pallas_call(
        functools.partial(_down_proj_body, tn=tn, rows=NORM_ROWS),
        grid=(m // tm, D_MODEL // tn),
        in_specs=[pl.BlockSpec((tm, D_FF), lambda i, j: (i, 0), pipeline_mode=pl.Buffered(1)),
                  pl.BlockSpec((D_FF, tn), lambda i, j: (0, j)),
                  pl.BlockSpec((tm, tn), lambda i, j: (i, j)),
                  pl.BlockSpec((1, D_MODEL), lambda i, j: (0, 0))],
        out_specs=pl.BlockSpec((tm, D_MODEL), lambda i, j: (i, 0)),
        out_shape=jax.ShapeDtypeStruct((m, D_MODEL), F32),
        compiler_params=_params("parallel", "arbitrary"),
        name="down_proj",
    )(act, w_down, x1, g)


def _cast_in_weights_body(wt_ref, wat_ref, wm_ref, wa_ref):
    wm_ref[...] = wt_ref[...].T.astype(BF16)

    @pl.when(pl.program_id(0) == 0)
    def _():
        wa_ref[...] = wat_ref[...].astype(BF16)


def _cast_in_weights(w_in_t, *, tn=512):
    a_lo = V_OFF + 2 * GLA_WIDTH
    src_row = lambda c: A_COLS * (c * (tn // A_COLS) + jnp.where(c * tn >= a_lo, 1, 0))
    return pl.pallas_call(
        _cast_in_weights_body,
        grid=(P_COLS // tn,),
        in_specs=[pl.BlockSpec((pl.Element(tn), pl.Element(D_MODEL)), lambda c: (src_row(c), 0)),
                  pl.BlockSpec((A_COLS, D_MODEL), lambda c: (a_lo // A_COLS, 0))],
        out_specs=[pl.BlockSpec((D_MODEL, tn), lambda c: (0, c)),
                   pl.BlockSpec((A_COLS, D_MODEL), lambda c: (0, 0))],
        out_shape=[jax.ShapeDtypeStruct((D_MODEL, P_COLS), BF16),
                   jax.ShapeDtypeStruct((A_COLS, D_MODEL), BF16)],
        compiler_params=_params("arbitrary"),
        name="cast_in_weights",
    )(w_in_t, w_in_t)


def _prepare_weights(mix_norm_g, w_in, w_gate2, b_gate2, head_norm_g, conv_mix_w, w_out,
                     ffn_norm_g, w_up, ffn_conv_w, ffn_conv_b, w_down, final_norm_g):
    w_main, w_a = _cast_in_weights(w_in.T)
    wf = w_gate2[0].reshape(GATE_RANK, GLA_HEADS, GLA_DK).transpose(1, 0, 2)
    wb = w_gate2[1].reshape(GATE_RANK, GLA_HEADS, GLA_DK).transpose(1, 0, 2)
    zero = jnp.zeros_like(wf)
    w2blk = jnp.concatenate([jnp.concatenate([wf, zero], axis=2),
                             jnp.concatenate([zero, wb], axis=2)], axis=1).astype(BF16)
    b2blk = jnp.concatenate([b_gate2[0].reshape(GLA_HEADS, 1, GLA_DK),
                             b_gate2[1].reshape(GLA_HEADS, 1, GLA_DK)], axis=2).astype(F32)
    return dict(
        mix_g=mix_norm_g.reshape(1, D_MODEL), w_main=w_main, w_a=w_a, w2blk=w2blk, b2blk=b2blk,
        head_g=head_norm_g.reshape(1, GLA_DV), conv_mix_w=conv_mix_w,
        w_out=w_out.astype(BF16),
        ffn_g=ffn_norm_g.reshape(1, D_MODEL), w_up=w_up.astype(BF16),
        ffn_conv_w=ffn_conv_w,
        ffn_conv_b=ffn_conv_b.reshape(1, D_FF), w_down=w_down.astype(BF16),
        final_g=final_norm_g.reshape(1, D_MODEL))


def kernel(x_prompt, x_sample, meta_tokens, mix_norm_g, w_in, w_gate2, b_gate2, head_norm_g,
           conv_mix_w, w_out, ffn_norm_g, w_up, ffn_conv_w, ffn_conv_b, w_down, final_norm_g):
    w = _prepare_weights(mix_norm_g[0], w_in[0], w_gate2[0], b_gate2[0], head_norm_g[0],
                         conv_mix_w[0], w_out[0], ffn_norm_g[0], w_up[0], ffn_conv_w[0],
                         ffn_conv_b[0], w_down[0], final_norm_g)
    trunks = (x_prompt, x_sample)
    p_meta16, a_meta16 = _in_proj(meta_tokens, w["mix_g"], w["w_main"], w["w_a"], tm=N_META)

    mixed = []
    for x in trunks:
        bsz, seq, _ = x.shape
        x_main = x.reshape(bsz * seq, D_MODEL)
        p_main, a_main = _in_proj(x_main, w["mix_g"], w["w_main"], w["w_a"], tm=1024)
        o_main, o_meta = _gla(p_main, a_main, p_meta16, a_meta16, w["w2blk"], w["b2blk"],
                              w["head_g"], bsz=bsz, seq=seq, shared_meta=True)
        c_main, c_meta = _mixconv(p_main, p_meta16, w["conv_mix_w"], bsz=bsz, seq=seq,
                                  shared_meta=True)
        mixed.append((x_main, o_main, c_main, o_meta, c_meta))

    n_seq = sum(x.shape[0] for x in trunks)
    _, h2_meta = _out_proj(jnp.concatenate([t[3] for t in mixed], axis=0),
                           jnp.concatenate([t[4] for t in mixed], axis=0),
                           w["w_out"], jnp.tile(meta_tokens, (n_seq, 1)), w["ffn_g"],
                           tm=n_seq * N_META)

    outs = []
    meta_off = 0
    for x, (x_main, o_main, c_main, _, _) in zip(trunks, mixed):
        bsz, seq, _ = x.shape
        x1, h2 = _out_proj(o_main, c_main, w["w_out"], x_main, w["ffn_g"], tm=512)
        act = _up_proj(h2, h2_meta, w["w_up"], w["ffn_conv_w"], w["ffn_conv_b"],
                       seq=seq, meta_off=meta_off, tm=1024)
        y = _down_proj(act, w["w_down"], x1, w["final_g"], tm=512)
        outs.append(y.reshape(bsz, seq, D_MODEL))
        meta_off += bsz
    return tuple(outs)
```
